```python
import math
import jax, jax.numpy as jnp
from jax import lax
import numpy as np

D_MODEL = 1024
BATCH = 2
SEQ = 8192
DEPTH = 2

CHUNK = 64
N_LEFT_CHUNKS = 8
BAND = (N_LEFT_CHUNKS + 1) * CHUNK
HEAD_DIM = 64
N_HEADS_A = 8
N_HEADS_B = 8
WIDTH_A = N_HEADS_A * HEAD_DIM
WIDTH_B = N_HEADS_B * HEAD_DIM
POOL_WINDOWS = (2, 4, 8, 16)
N_POOL_GROUPS = len(POOL_WINDOWS)
POOL_GROUP_DIM = D_MODEL // 8
WIDTH_C = N_POOL_GROUPS * POOL_GROUP_DIM
N_BRANCHES = 3
MAX_REL_DIST = 2 * CHUNK
REL_TABLE = MAX_REL_DIST + CHUNK
D_FF = 2816
CONV_WIDTH = 3
Q_BLOCK = 128
EPS = 1e-6

IN_SIZES = (WIDTH_A, WIDTH_A, WIDTH_A, WIDTH_B, WIDTH_B, WIDTH_B, WIDTH_C,
            N_BRANCHES * D_MODEL)
IN_COLS = sum(IN_SIZES)
IN_SPLITS = [int(v) for v in np.cumsum(IN_SIZES)[:-1]]

kernel_name = "hybrid_chunk_causal_gated_branches"


def rms_norm(x, gain):
    xf = x.astype(jnp.float32)
    y = xf * lax.rsqrt(jnp.mean(xf * xf, axis=-1, keepdims=True) + EPS)
    return (y * gain.astype(jnp.float32)).astype(x.dtype)


def chunked_rel_attention(q, k, v, g_q, g_k, rel_bias):
    B, S, H, Dh = q.shape
    nc = S // CHUNK
    q = rms_norm(q, g_q)
    k = rms_norm(k, g_k)
    qc = q.reshape(B, nc, CHUNK, H, Dh)
    pad = ((0, 0), (N_LEFT_CHUNKS, 0), (0, 0), (0, 0), (0, 0))
    kp = jnp.pad(k.reshape(B, nc, CHUNK, H, Dh), pad)
    vp = jnp.pad(v.reshape(B, nc, CHUNK, H, Dh), pad)
    k_band = jnp.concatenate([kp[:, j:j + nc] for j in range(N_LEFT_CHUNKS + 1)], axis=2)
    v_band = jnp.concatenate([vp[:, j:j + nc] for j in range(N_LEFT_CHUNKS + 1)], axis=2)
    logits = jnp.einsum('bcqhd,bckhd->bhcqk', qc, k_band).astype(jnp.float32) / math.sqrt(Dh)
    q_off = jnp.arange(CHUNK)[:, None] + N_LEFT_CHUNKS * CHUNK
    k_off = jnp.arange(BAND)[None, :]
    rel = jnp.clip(q_off - k_off, -(CHUNK - 1), MAX_REL_DIST) + (CHUNK - 1)
    bias = rel_bias.astype(jnp.float32)[:, rel]
    valid = (jnp.arange(nc)[:, None] + k_off // CHUNK - N_LEFT_CHUNKS) >= 0
    logits = jnp.where(valid[None, None, :, None, :], logits + bias[None, :, None], -1e30)
    p = jax.nn.softmax(logits, axis=-1)
    out = jnp.einsum('bhcqk,bckhd->bcqhd', p.astype(v.dtype), v_band)
    return out.reshape(B, S, H * Dh)


def stick_breaking_attention(q, k, v):
    B, S, H, Dh = q.shape
    nb = S // Q_BLOCK
    scale = 1.0 / math.sqrt(Dh)
    qb = q.reshape(B, nb, Q_BLOCK, H, Dh).transpose(1, 0, 2, 3, 4)
    k_pos = jnp.arange(S)
    starts = jnp.arange(nb, dtype=jnp.int32) * Q_BLOCK

    def block(args):
        q_blk, start = args
        z = jnp.einsum('bqhd,bkhd->bhqk', q_blk, k).astype(jnp.float32) * scale
        q_pos = start + jnp.arange(Q_BLOCK)
        before = k_pos[None, :] < q_pos[:, None]
        log_keep = jnp.where(before, jax.nn.log_sigmoid(-z), 0.0)
        tail = lax.cumsum(log_keep, axis=3, reverse=True) - log_keep
        w = jnp.where(before, jnp.exp(jax.nn.log_sigmoid(z) + tail), 0.0)
        return jnp.einsum('bhqk,bkhd->bqhd', w.astype(v.dtype), v)

    out = lax.map(block, (qb, starts))
    return out.transpose(1, 0, 2, 3, 4).reshape(B, S, H * Dh)


def multiscale_pool(u, w_group, scale):
    B, S, _ = u.shape
    uf = u.astype(jnp.float32).reshape(B, S, N_POOL_GROUPS, POOL_GROUP_DIM)
    cs = jnp.pad(jnp.cumsum(uf, axis=1), ((0, 0), (1, 0), (0, 0), (0, 0)))
    t = jnp.arange(S)
    pooled = []
    for g, win in enumerate(POOL_WINDOWS):
        lo = jnp.maximum(t + 1 - win, 0)
        win_sum = cs[:, 1:, g] - cs[:, lo, g]
        count = (t + 1 - lo).astype(jnp.float32)
        pooled.append(win_sum / count[None, :, None])
    pooled = jnp.stack(pooled, axis=2) - uf
    mixed = jnp.einsum('bsgc,gce->bsge', pooled.astype(u.dtype), w_group)
    return mixed.reshape(B, S, WIDTH_C) * scale


def conv_gated_mlp(h, w_up, conv_w, conv_b, w_down):
    S = h.shape[1]
    u = h @ w_up
    up = jnp.pad(u, ((0, 0), (CONV_WIDTH - 1, 0), (0, 0)))
    c = conv_b + conv_w[0] * up[:, 0:S]
    for j in range(1, CONV_WIDTH):
        c = c + conv_w[j] * up[:, j:j + S]
    gate, val = jnp.split(c, 2, axis=-1)
    return (jax.nn.silu(gate) * val) @ w_down


def setup_inputs(seed: int = 0) -> dict:
    key = jax.random.key(seed)
    ks = jax.random.split(key, 20)
    f32 = jnp.float32
    n = lambda k, shape, s: jax.random.normal(k, shape, f32) * s
    L = DEPTH
    return {
        "x": n(ks[0], (BATCH, SEQ, D_MODEL), 1.0),
        "norm_mix": 1.0 + n(ks[1], (L, D_MODEL), 0.05),
        "w_in": n(ks[2], (L, D_MODEL, IN_COLS), D_MODEL ** -0.5),
        "b_gate": n(ks[3], (L, N_BRANCHES * D_MODEL), 0.1),
        "q_norm_a": 1.0 + n(ks[4], (L, HEAD_DIM), 0.05),
        "k_norm_a": 1.0 + n(ks[5], (L, HEAD_DIM), 0.05),
        "rel_bias_a": n(ks[6], (L, N_HEADS_A, REL_TABLE), 0.5),
        "w_pool": n(ks[7], (L, N_POOL_GROUPS, POOL_GROUP_DIM, POOL_GROUP_DIM), POOL_GROUP_DIM ** -0.5),
        "pool_scale": 1.0 + n(ks[8], (L, WIDTH_C), 0.1),
        "w_branch_a": n(ks[9], (L, WIDTH_A, D_MODEL), WIDTH_A ** -0.5),
        "w_branch_b": n(ks[10], (L, WIDTH_B, D_MODEL), WIDTH_B ** -0.5),
        "w_branch_c": n(ks[11], (L, WIDTH_C, D_MODEL), WIDTH_C ** -0.5),
        "w_out": n(ks[12], (L, D_MODEL, D_MODEL), D_MODEL ** -0.5),
        "norm_ffn": 1.0 + n(ks[13], (L, D_MODEL), 0.05),
        "w_up": n(ks[14], (L, D_MODEL, 2 * D_FF), D_MODEL ** -0.5),
        "conv_w": n(ks[15], (L, CONV_WIDTH, 2 * D_FF), CONV_WIDTH ** -0.5),
        "conv_b": n(ks[16], (L, 2 * D_FF), 0.02),
        "w_down": n(ks[17], (L, D_FF, D_MODEL), D_FF ** -0.5),
    }


def reference(x, norm_mix, w_in, b_gate, q_norm_a, k_norm_a, rel_bias_a, w_pool,
              pool_scale, w_branch_a, w_branch_b, w_branch_c, w_out, norm_ffn,
              w_up, conv_w, conv_b, w_down):
    B, S, D = x.shape
    for l in range(DEPTH):
        h = rms_norm(x, norm_mix[l])
        proj = h @ w_in[l]
        q_a, k_a, v_a, q_b, k_b, v_b, u_c, g_logits = jnp.split(proj, IN_SPLITS, axis=-1)
        heads_a = lambda t: t.reshape(B, S, N_HEADS_A, HEAD_DIM)
        heads_b = lambda t: t.reshape(B, S, N_HEADS_B, HEAD_DIM)
        o_a = chunked_rel_attention(heads_a(q_a), heads_a(k_a), heads_a(v_a),
                                    q_norm_a[l], k_norm_a[l], rel_bias_a[l])
        o_b = stick_breaking_attention(heads_b(q_b), heads_b(k_b), heads_b(v_b))
        o_c = multiscale_pool(u_c, w_pool[l], pool_scale[l])
        gates = jax.nn.sigmoid((g_logits + b_gate[l]).astype(jnp.float32)).astype(x.dtype)
        gates = gates.reshape(B, S, N_BRANCHES, D)
        merged = (gates[:, :, 0] * (o_a @ w_branch_a[l])
                  + gates[:, :, 1] * (o_b @ w_branch_b[l])
                  + gates[:, :, 2] * (o_c @ w_branch_c[l]))
        x = x + merged @ w_out[l]
        h2 = rms_norm(x, norm_ffn[l])
        x = x + conv_gated_mlp(h2, w_up[l], conv_w[l], conv_b[l], w_down[l])
    return x
```

```python
import functools
import math

import jax
import jax.numpy as jnp
import numpy as np
from jax import lax
from jax.experimental import pallas as pl
from jax.experimental.pallas import tpu as pltpu

D_MODEL = 1024
CHUNK = 64
N_LEFT_CHUNKS = 8
BAND = (N_LEFT_CHUNKS + 1) * CHUNK
HEAD_DIM = 64
N_HEADS = 8
WIDTH = N_HEADS * HEAD_DIM
POOL_WINDOWS = (2, 4, 8, 16)
POOL_GROUP_DIM = 128
MAX_REL_DIST = 2 * CHUNK
D_FF = 2816
EPS = 1e-6
MASK_VALUE = -1e30

QKVU_COLS = 7 * WIDTH
GATE_COLS = 3 * D_MODEL

LANES = 128
BF16_SUBLANES = 16
VMEM_LIMIT_BYTES = 56 * 1024 * 1024

TM_PROJ = 512
TN_PROJ = 512
TQ_A = N_LEFT_CHUNKS * CHUNK
TQ_B = 256
KB_B = 256
FF_CHUNK = 256
HALO_POOL = 16
HALO_CONV = 8

BF16 = jnp.bfloat16
F32 = jnp.float32


def _dot(a, b):
    return jnp.dot(a, b, preferred_element_type=F32)


def _dot_nt(a, b):
    return lax.dot_general(a, b, (((1,), (1,)), ((), ())), preferred_element_type=F32)


def _rms_norm_rows(x, gain):
    ms = jnp.mean(x * x, axis=-1, keepdims=True)
    return x * lax.rsqrt(ms + EPS) * gain


def _compiler_params(semantics):
    return pltpu.CompilerParams(dimension_semantics=semantics,
                                vmem_limit_bytes=VMEM_LIMIT_BYTES)


def _resident(shape):
    zeros = (0,) * len(shape)
    return pl.BlockSpec(shape, lambda *_: zeros, pipeline_mode=pl.Buffered(1))


def _in_proj_kernel(x_ref, g_ref, w_ref, b_ref, qkvu_ref, gate_ref):
    h = _rms_norm_rows(x_ref[...], g_ref[...]).astype(BF16)
    for n0 in range(0, QKVU_COLS, TN_PROJ):
        qkvu_ref[:, n0:n0 + TN_PROJ] = _dot(h, w_ref[:, n0:n0 + TN_PROJ]).astype(BF16)
    for n0 in range(0, GATE_COLS, TN_PROJ):
        z = _dot(h, w_ref[:, QKVU_COLS + n0:QKVU_COLS + n0 + TN_PROJ]) + b_ref[:, n0:n0 + TN_PROJ]
        gate_ref[:, n0:n0 + TN_PROJ] = jax.nn.sigmoid(z).astype(BF16)


def _in_proj(x2, gain, w_in, b_gate):
    t = x2.shape[0]
    return pl.pallas_call(
        _in_proj_kernel,
        grid=(t // TM_PROJ,),
        in_specs=[
            pl.BlockSpec((TM_PROJ, D_MODEL), lambda i: (i, 0)),
            _resident((1, D_MODEL)),
            _resident((D_MODEL, QKVU_COLS + GATE_COLS)),
            _resident((1, GATE_COLS)),
        ],
        out_specs=[
            pl.BlockSpec((TM_PROJ, QKVU_COLS), lambda i: (i, 0)),
            pl.BlockSpec((TM_PROJ, GATE_COLS), lambda i: (i, 0)),
        ],
        out_shape=[
            jax.ShapeDtypeStruct((t, QKVU_COLS), BF16),
            jax.ShapeDtypeStruct((t, GATE_COLS), BF16),
        ],
        compiler_params=_compiler_params(("arbitrary",)),
        name="in_proj",
    )(x2, gain, w_in, b_gate)


def _head_pair_norm(x, gain, low_lanes):
    sq = x * x
    s_low = jnp.sum(jnp.where(low_lanes, sq, 0.0), axis=-1, keepdims=True)
    s_high = jnp.sum(jnp.where(low_lanes, 0.0, sq), axis=-1, keepdims=True)
    ms = jnp.where(low_lanes, s_low, s_high) * (1.0 / HEAD_DIM)
    return x * lax.rsqrt(ms + EPS) * gain


def _mixer_a_kernel(q_ref, kp_ref, kc_ref, vp_ref, vc_ref, gq_ref, gk_ref, bias_ref,
                    o_ref, k_scr, v_scr):
    t = pl.program_id(2)
    low_lanes = lax.broadcasted_iota(jnp.int32, (1, LANES), 1) < HEAD_DIM

    q = _head_pair_norm(q_ref[0].astype(F32), gq_ref[...], low_lanes) * (1.0 / math.sqrt(HEAD_DIM))
    q = q.astype(BF16)

    for half, k_in in enumerate((kp_ref, kc_ref)):
        kn = _head_pair_norm(k_in[0].astype(F32), gk_ref[...], low_lanes)
        rows = slice(half * TQ_A, (half + 1) * TQ_A)
        k_scr[0, rows, :] = jnp.where(low_lanes, kn, 0.0).astype(BF16)
        k_scr[1, rows, :] = jnp.where(low_lanes, 0.0, kn).astype(BF16)
    v_scr[0:TQ_A, :] = vp_ref[0]
    v_scr[TQ_A:2 * TQ_A, :] = vc_ref[0]

    key_idx = lax.broadcasted_iota(jnp.int32, (1, BAND), 1)
    for c in range(TQ_A // CHUNK):
        r0 = c * CHUNK
        qc = q[r0:r0 + CHUNK]
        before_start = jnp.logical_and(key_idx + r0 < TQ_A, t == 0)
        start_mask = jnp.where(before_start, MASK_VALUE, 0.0)
        outs = []
        for h in range(2):
            logits = _dot_nt(qc, k_scr[h, r0:r0 + BAND, :]) + bias_ref[h] + start_mask
            m = jnp.max(logits, axis=-1, keepdims=True)
            p = jnp.exp(logits - m)
            denom = jnp.sum(p, axis=-1, keepdims=True)
            pv = _dot(p.astype(BF16), v_scr[r0:r0 + BAND, :])
            outs.append(pv / denom)
        o_ref[0, r0:r0 + CHUNK, :] = jnp.where(low_lanes, outs[0], outs[1]).astype(BF16)


def _mixer_a(qkvu3, gq, gk, bias):
    b, s, _ = qkvu3.shape
    pairs = N_HEADS // 2
    blk = (1, TQ_A, LANES)
    prev = lambda t: jnp.maximum(t - 1, 0)
    return pl.pallas_call(
        _mixer_a_kernel,
        grid=(b, pairs, s // TQ_A),
        in_specs=[
            pl.BlockSpec(blk, lambda bi, p, t: (bi, t, p)),
            pl.BlockSpec(blk, lambda bi, p, t: (bi, prev(t), pairs + p)),
            pl.BlockSpec(blk, lambda bi, p, t: (bi, t, pairs + p)),
            pl.BlockSpec(blk, lambda bi, p, t: (bi, prev(t), 2 * pairs + p)),
            pl.BlockSpec(blk, lambda bi, p, t: (bi, t, 2 * pairs + p)),
            _resident((1, LANES)),
            _resident((1, LANES)),
            pl.BlockSpec((2, CHUNK, BAND), lambda bi, p, t: (p, 0, 0)),
        ],
        out_specs=pl.BlockSpec(blk, lambda bi, p, t: (bi, t, p)),
        out_shape=jax.ShapeDtypeStruct((b, s, WIDTH), BF16),
        scratch_shapes=[
            pltpu.VMEM((2, 2 * TQ_A, LANES), BF16),
            pltpu.VMEM((2 * TQ_A, LANES), BF16),
        ],
        compiler_params=_compiler_params(("arbitrary", "arbitrary", "arbitrary")),
        name="mixer_a",
    )(qkvu3, qkvu3, qkvu3, qkvu3, qkvu3, gq, gk, bias)


def _mixer_b_kernel(q_ref, k_ref, v_ref, cum_ref, o_ref, acc_ref, carry_ref):
    i = pl.program_id(2)
    low_lanes = lax.broadcasted_iota(jnp.int32, (1, LANES), 1) < HEAD_DIM
    q = (q_ref[0].astype(F32) * (1.0 / math.sqrt(HEAD_DIM))).astype(BF16)

    acc_ref[...] = jnp.zeros_like(acc_ref)
    carry_ref[...] = jnp.zeros_like(carry_ref)

    def visit(j, diagonal):
        start = pl.multiple_of(j * KB_B, KB_B)
        k_blk = k_ref[0, pl.ds(start, KB_B), :]
        v_blk = v_ref[0, pl.ds(start, KB_B), :]
        if diagonal:
            row = lax.broadcasted_iota(jnp.int32, (TQ_B, KB_B), 0)
            col = lax.broadcasted_iota(jnp.int32, (TQ_B, KB_B), 1)
            before = col < row
        for h in range(2):
            k_h = jnp.where(low_lanes, k_blk, 0) if h == 0 else jnp.where(low_lanes, 0, k_blk)
            z = _dot_nt(q, k_h.astype(BF16))
            log_beta = jnp.minimum(z, 0.0) - jnp.log1p(jnp.exp(-jnp.abs(z)))
            log_keep = log_beta - z
            if diagonal:
                log_keep = jnp.where(before, log_keep, 0.0)
            hi = log_keep.astype(BF16)
            lo = (log_keep - hi.astype(F32)).astype(BF16)
            sums = _dot(hi, cum_ref[...]) + _dot(lo, cum_ref[...])
            tail = sums[:, :KB_B]
            block_total = sums[:, KB_B:]
            carry = carry_ref[h]
            w = jnp.exp(log_beta + tail + jnp.concatenate([carry] * (KB_B // LANES), axis=1))
            if diagonal:
                w = jnp.where(before, w, 0.0)
            acc_ref[h] += _dot(w.astype(BF16), v_blk)
            carry_ref[h] = carry + block_total

    visit(i, True)

    def body(step, _):
        visit(i - 1 - step, False)
        return 0

    lax.fori_loop(0, i, body, 0)
    o_ref[0] = jnp.where(low_lanes, acc_ref[0], acc_ref[1]).astype(BF16)


def _cumulative_matrix():
    j = np.arange(KB_B)[:, None]
    s = np.arange(KB_B)[None, :]
    strict = (j > s).astype(np.float32)
    return jnp.asarray(np.concatenate([strict, np.ones((KB_B, LANES), np.float32)], axis=1), BF16)


def _mixer_b(qkvu3):
    b, s, _ = qkvu3.shape
    pairs = N_HEADS // 2
    return pl.pallas_call(
        _mixer_b_kernel,
        grid=(b, pairs, s // TQ_B),
        in_specs=[
            pl.BlockSpec((1, TQ_B, LANES), lambda bi, p, i: (bi, i, 3 * pairs + p)),
            pl.BlockSpec((1, s, LANES), lambda bi, p, i: (bi, 0, 4 * pairs + p)),
            pl.BlockSpec((1, s, LANES), lambda bi, p, i: (bi, 0, 5 * pairs + p)),
            _resident((KB_B, KB_B + LANES)),
        ],
        out_specs=pl.BlockSpec((1, TQ_B, LANES), lambda bi, p, i: (bi, i, p)),
        out_shape=jax.ShapeDtypeStruct((b, s, WIDTH), BF16),
        scratch_shapes=[
            pltpu.VMEM((2, TQ_B, LANES), F32),
            pltpu.VMEM((2, TQ_B, LANES), F32),
        ],
        compiler_params=_compiler_params(("arbitrary", "arbitrary", "arbitrary")),
        name="mixer_b",
    )(qkvu3, qkvu3, qkvu3, _cumulative_matrix())


def _merge_kernel(tiles_per_seq, x_ref, oa_ref, ob_ref, u_ref, uh_ref, gate_ref,
                  wp_ref, ps_ref, wa_ref, wb_ref, wc_ref, wo_ref, o_ref):
    first = (pl.program_id(0) % tiles_per_seq) == 0
    u = u_ref[...].astype(F32)
    halo = jnp.where(first, 0.0, uh_ref[...].astype(F32))
    ext = jnp.concatenate([halo, u], axis=0)
    pos = lax.broadcasted_iota(jnp.int32, (TM_PROJ, 1), 0)

    pooled_groups = []
    for g, win in enumerate(POOL_WINDOWS):
        cols = slice(g * POOL_GROUP_DIM, (g + 1) * POOL_GROUP_DIM)
        a = ext[:, cols]
        step = 1
        while step < win:
            a = a + pltpu.roll(a, step, 0)
            step *= 2
        win_sum = a[HALO_POOL:]
        count = jnp.where(first, jnp.minimum(pos + 1, win), win).astype(F32)
        pooled = win_sum / count - u[:, cols]
        mixed = _dot(pooled.astype(BF16), wp_ref[g])
        pooled_groups.append(mixed * ps_ref[:, cols])
    o_c = jnp.concatenate(pooled_groups, axis=1).astype(BF16)

    merged = (gate_ref[:, 0:D_MODEL].astype(F32) * _dot(oa_ref[...], wa_ref[...])
              + gate_ref[:, D_MODEL:2 * D_MODEL].astype(F32) * _dot(ob_ref[...], wb_ref[...])
              + gate_ref[:, 2 * D_MODEL:3 * D_MODEL].astype(F32) * _dot(o_c, wc_ref[...]))
    o_ref[...] = x_ref[...] + _dot(merged.astype(BF16), wo_ref[...])


def _merge(x2, o_a, o_b, qkvu, gates, w_pool, pool_scale, w_a, w_b, w_c, w_out, seq):
    t = x2.shape[0]
    u_block = QKVU_COLS // WIDTH - 1
    halo_blocks = TM_PROJ // HALO_POOL
    return pl.pallas_call(
        functools.partial(_merge_kernel, seq // TM_PROJ),
        grid=(t // TM_PROJ,),
        in_specs=[
            pl.BlockSpec((TM_PROJ, D_MODEL), lambda i: (i, 0)),
            pl.BlockSpec((TM_PROJ, WIDTH), lambda i: (i, 0)),
            pl.BlockSpec((TM_PROJ, WIDTH), lambda i: (i, 0)),
            pl.BlockSpec((TM_PROJ, WIDTH), lambda i: (i, u_block)),
            pl.BlockSpec((HALO_POOL, WIDTH),
                         lambda i: (jnp.maximum(i * halo_blocks - 1, 0), u_block * (WIDTH // WIDTH))),
            pl.BlockSpec((TM_PROJ, GATE_COLS), lambda i: (i, 0)),
            _resident((len(POOL_WINDOWS), POOL_GROUP_DIM, POOL_GROUP_DIM)),
            _resident((1, WIDTH)),
            _resident((WIDTH, D_MODEL)),
            _resident((WIDTH, D_MODEL)),
            _resident((WIDTH, D_MODEL)),
            _resident((D_MODEL, D_MODEL)),
        ],
        out_specs=pl.BlockSpec((TM_PROJ, D_MODEL), lambda i: (i, 0)),
        out_shape=jax.ShapeDtypeStruct((t, D_MODEL), F32),
        compiler_params=_compiler_params(("arbitrary",)),
        name="merge",
    )(x2, o_a, o_b, qkvu, qkvu, gates, w_pool, pool_scale, w_a, w_b, w_c, w_out)


def _ffn_kernel(tiles_per_seq, x_ref, g_ref, wu_ref, cw_ref, cb_ref, wd_ref, o_ref, halo_ref):
    first = (pl.program_id(0) % tiles_per_seq) == 0
    x = x_ref[...]
    h = _rms_norm_rows(x, g_ref[...]).astype(BF16)
    row = lax.broadcasted_iota(jnp.int32, (TM_PROJ, 1), 0)

    def conv(cols):
        u = _dot(h, wu_ref[:, cols])
        prev = jnp.where(first, 0.0, halo_ref[:, cols])
        halo_ref[:, cols] = u[TM_PROJ - HALO_CONV:]
        back1 = prev[HALO_CONV - 1:HALO_CONV]
        back2 = prev[HALO_CONV - 2:HALO_CONV - 1]
        u1 = jnp.where(row == 0, back1, pltpu.roll(u, 1, 0))
        u2 = jnp.where(row == 0, back2, jnp.where(row == 1, back1, pltpu.roll(u, 2, 0)))
        return (cb_ref[:, cols] + cw_ref[0:1, cols] * u2 + cw_ref[1:2, cols] * u1
                + cw_ref[2:3, cols] * u)

    acc = x
    for n0 in range(0, D_FF, FF_CHUNK):
        gate = conv(slice(n0, n0 + FF_CHUNK))
        val = conv(slice(D_FF + n0, D_FF + n0 + FF_CHUNK))
        act = (jax.nn.silu(gate) * val).astype(BF16)
        acc = acc + _dot(act, wd_ref[n0:n0 + FF_CHUNK, :])
    o_ref[...] = acc


def _ffn(x2, gain, w_up, conv_w, conv_b, w_down, seq):
    t = x2.shape[0]
    return pl.pallas_call(
        functools.partial(_ffn_kernel, seq // TM_PROJ),
        grid=(t // TM_PROJ,),
        in_specs=[
            pl.BlockSpec((TM_PROJ, D_MODEL), lambda i: (i, 0)),
            _resident((1, D_MODEL)),
            _resident((D_MODEL, 2 * D_FF)),
            _resident((3, 2 * D_FF)),
            _resident((1, 2 * D_FF)),
            _resident((D_FF, D_MODEL)),
        ],
        out_specs=pl.BlockSpec((TM_PROJ, D_MODEL), lambda i: (i, 0)),
        out_shape=jax.ShapeDtypeStruct((t, D_MODEL), F32),
        scratch_shapes=[pltpu.VMEM((HALO_CONV, 2 * D_FF), F32)],
        compiler_params=_compiler_params(("arbitrary",)),
        name="ffn",
    )(x2, gain, w_up, conv_w, conv_b, w_down)


def _rel_bias_table(rel_bias):
    q_off = np.arange(CHUNK)[:, None] + N_LEFT_CHUNKS * CHUNK
    k_off = np.arange(BAND)[None, :]
    rel = np.clip(q_off - k_off, -(CHUNK - 1), MAX_REL_DIST) + (CHUNK - 1)
    return rel_bias.astype(F32)[:, rel]


def kernel(x, norm_mix, w_in, b_gate, q_norm_a, k_norm_a, rel_bias_a, w_pool, pool_scale,
           w_branch_a, w_branch_b, w_branch_c, w_out, norm_ffn, w_up, conv_w, conv_b, w_down):
    b, s, d = x.shape
    assert d == D_MODEL and s % TQ_A == 0 and s % TM_PROJ == 0 and s % TQ_B == 0
    depth = w_in.shape[0]
    x2 = x.reshape(b * s, d)
    for l in range(depth):
        qkvu, gates = _in_proj(x2, norm_mix[l][None, :], w_in[l].astype(BF16), b_gate[l][None, :])
        qkvu3 = qkvu.reshape(b, s, QKVU_COLS)
        gq = jnp.tile(q_norm_a[l], 2)[None, :]
        gk = jnp.tile(k_norm_a[l], 2)[None, :]
        o_a = _mixer_a(qkvu3, gq, gk, _rel_bias_table(rel_bias_a[l]))
        o_b = _mixer_b(qkvu3)
        x2 = _merge(x2, o_a.reshape(b * s, WIDTH), o_b.reshape(b * s, WIDTH), qkvu, gates,
                    w_pool[l].astype(BF16), pool_scale[l][None, :],
                    w_branch_a[l].astype(BF16), w_branch_b[l].astype(BF16),
                    w_branch_c[l].astype(BF16), w_out[l].astype(BF16), s)
        x2 = _ffn(x2, norm_ffn[l][None, :], w_up[l].astype(BF16), conv_w[l], conv_b[l][None, :],
                  w_down[l].astype(BF16), s)
    return x2.reshape(b, s, d)
```

```python
import functools
import math

import jax
import jax.numpy as jnp
import numpy as np
from jax import lax
from jax.experimental import pallas as pl
from jax.experimental.pallas import tpu as pltpu

D_MODEL = 1024
CHUNK = 64
N_LEFT_CHUNKS = 8
HEAD_DIM = 64
N_HEADS = 8
N_PAIRS = N_HEADS // 2
WIDTH = N_HEADS * HEAD_DIM
POOL_WINDOWS = (2, 4, 8, 16)
POOL_GROUP_DIM = 128
MAX_REL_DIST = 2 * CHUNK
D_FF = 2816
EPS = 1e-6
MASK_VALUE = -1e30

QKVU_COLS = 7 * WIDTH
GATE_COLS = 3 * D_MODEL

LANES = 128
VMEM_LIMIT_BYTES = 56 * 1024 * 1024

TM_PROJ = 512
TN_PROJ = 512
TQ_A = N_LEFT_CHUNKS * CHUNK
UNIT_A = 2 * CHUNK
BAND_A = (N_LEFT_CHUNKS + 2) * CHUNK
TQ_B = 256
KB_B = 256
FF_CHUNK = 256
HALO_POOL = 16
HALO_CONV = 8

LOG2E = 1.4426950408889634

BF16 = jnp.bfloat16
F32 = jnp.float32


def _dot(a, b):
    return jnp.dot(a, b, preferred_element_type=F32)


def _rms_norm_rows(x, gain):
    ms = jnp.mean(x * x, axis=-1, keepdims=True)
    return x * lax.rsqrt(ms + EPS) * gain


def _compiler_params(semantics):
    return pltpu.CompilerParams(dimension_semantics=semantics,
                                vmem_limit_bytes=VMEM_LIMIT_BYTES)


def _resident(shape):
    zeros = (0,) * len(shape)
    return pl.BlockSpec(shape, lambda *_: zeros, pipeline_mode=pl.Buffered(1))


def _low_lanes():
    return lax.broadcasted_iota(jnp.int32, (1, LANES), 1) < HEAD_DIM


def _in_proj_kernel(x_ref, g_ref, w_ref, b_ref, qkvu_ref, gate_ref):
    h = _rms_norm_rows(x_ref[...], g_ref[...]).astype(BF16)
    for n0 in range(0, QKVU_COLS, TN_PROJ):
        qkvu_ref[:, n0:n0 + TN_PROJ] = _dot(h, w_ref[:, n0:n0 + TN_PROJ]).astype(BF16)
    for n0 in range(0, GATE_COLS, TN_PROJ):
        z = _dot(h, w_ref[:, QKVU_COLS + n0:QKVU_COLS + n0 + TN_PROJ]) + b_ref[:, n0:n0 + TN_PROJ]
        gate_ref[:, n0:n0 + TN_PROJ] = jax.nn.sigmoid(z).astype(BF16)


def _in_proj(x2, gain, w_in, b_gate):
    t = x2.shape[0]
    return pl.pallas_call(
        _in_proj_kernel,
        grid=(t // TM_PROJ,),
        in_specs=[
            pl.BlockSpec((TM_PROJ, D_MODEL), lambda i: (i, 0)),
            _resident((1, D_MODEL)),
            _resident((D_MODEL, QKVU_COLS + GATE_COLS)),
            _resident((1, GATE_COLS)),
        ],
        out_specs=[
            pl.BlockSpec((TM_PROJ, QKVU_COLS), lambda i: (i, 0)),
            pl.BlockSpec((TM_PROJ, GATE_COLS), lambda i: (i, 0)),
        ],
        out_shape=[
            jax.ShapeDtypeStruct((t, QKVU_COLS), BF16),
            jax.ShapeDtypeStruct((t, GATE_COLS), BF16),
        ],
        compiler_params=_compiler_params(("arbitrary",)),
        name="in_proj",
    )(x2, gain, w_in, b_gate)


def _head_pair_norm(x, gain, low_lanes):
    sq = x * x
    s_low = jnp.sum(jnp.where(low_lanes, sq, 0.0), axis=-1, keepdims=True)
    s_high = jnp.sum(jnp.where(low_lanes, 0.0, sq), axis=-1, keepdims=True)
    ms = jnp.where(low_lanes, s_low, s_high) * (1.0 / HEAD_DIM)
    return x * lax.rsqrt(ms + EPS) * gain


def _mixer_a_kernel(q_ref, kp_ref, kc_ref, vp_ref, vc_ref, gq_ref, gk_ref, bias_ref,
                    o_ref, q_scr, kt_scr, v_scr):
    t = pl.program_id(2)
    low_lanes = _low_lanes()
    n_units = TQ_A // UNIT_A

    q = _head_pair_norm(q_ref[0].astype(F32), gq_ref[...], low_lanes) * (1.0 / math.sqrt(HEAD_DIM))
    for u in range(n_units):
        qu = q[u * UNIT_A:(u + 1) * UNIT_A]
        q_scr[u, 0:UNIT_A, :] = jnp.where(low_lanes, qu, 0.0).astype(BF16)
        q_scr[u, UNIT_A:2 * UNIT_A, :] = jnp.where(low_lanes, 0.0, qu).astype(BF16)

    for half, k_in in enumerate((kp_ref, kc_ref)):
        kn = _head_pair_norm(k_in[0].astype(F32), gk_ref[...], low_lanes)
        kt_scr[:, half * TQ_A:(half + 1) * TQ_A] = kn.T.astype(BF16)
    v_scr[0:TQ_A, :] = vp_ref[0]
    v_scr[TQ_A:2 * TQ_A, :] = vc_ref[0]

    key_idx = lax.broadcasted_iota(jnp.int32, (1, BAND_A), 1)
    for u in range(n_units):
        k0 = u * UNIT_A
        before_start = jnp.logical_and(key_idx + k0 < TQ_A, t == 0)
        start_mask = jnp.where(before_start, MASK_VALUE, 0.0)
        logits = _dot(q_scr[u], kt_scr[:, k0:k0 + BAND_A]) + bias_ref[0] + start_mask
        m = jnp.max(logits, axis=-1, keepdims=True)
        p = jnp.exp(logits - m)
        denom = jnp.sum(p, axis=-1, keepdims=True)
        pv = _dot(p.astype(BF16), v_scr[k0:k0 + BAND_A, :]) / denom
        o_ref[0, k0:k0 + UNIT_A, :] = jnp.where(low_lanes, pv[0:UNIT_A], pv[UNIT_A:]).astype(BF16)


def _mixer_a(qkvu3, gq, gk, bias):
    b, s, _ = qkvu3.shape
    blk = (1, TQ_A, LANES)
    prev = lambda t: jnp.maximum(t - 1, 0)
    return pl.pallas_call(
        _mixer_a_kernel,
        grid=(b, N_PAIRS, s // TQ_A),
        in_specs=[
            pl.BlockSpec(blk, lambda bi, p, t: (bi, t, p)),
            pl.BlockSpec(blk, lambda bi, p, t: (bi, prev(t), N_PAIRS + p)),
            pl.BlockSpec(blk, lambda bi, p, t: (bi, t, N_PAIRS + p)),
            pl.BlockSpec(blk, lambda bi, p, t: (bi, prev(t), 2 * N_PAIRS + p)),
            pl.BlockSpec(blk, lambda bi, p, t: (bi, t, 2 * N_PAIRS + p)),
            _resident((1, LANES)),
            _resident((1, LANES)),
            pl.BlockSpec((1, 2 * UNIT_A, BAND_A), lambda bi, p, t: (p, 0, 0)),
        ],
        out_specs=pl.BlockSpec(blk, lambda bi, p, t: (bi, t, p)),
        out_shape=jax.ShapeDtypeStruct((b, s, WIDTH), BF16),
        scratch_shapes=[
            pltpu.VMEM((TQ_A // UNIT_A, 2 * UNIT_A, LANES), BF16),
            pltpu.VMEM((LANES, 2 * TQ_A), BF16),
            pltpu.VMEM((2 * TQ_A, LANES), BF16),
        ],
        compiler_params=_compiler_params(("arbitrary", "arbitrary", "arbitrary")),
        name="mixer_a",
    )(qkvu3, qkvu3, qkvu3, qkvu3, qkvu3, gq, gk, bias)


def _rel_bias_units(rel_bias):
    period = UNIT_A + BAND_A
    d = np.arange(period)
    d = np.where(d < BAND_A, d, d - period)
    idx = np.clip(N_LEFT_CHUNKS * CHUNK - d, -(CHUNK - 1), MAX_REL_DIST) + (CHUNK - 1)
    vec = rel_bias.astype(F32)[:, idx]
    h = rel_bias.shape[0]
    rows = jnp.tile(vec, (1, UNIT_A))[:, :UNIT_A * (period - 1)]
    toeplitz = rows.reshape(h, UNIT_A, period - 1)[:, :, :BAND_A]
    q_chunk = np.arange(UNIT_A)[:, None] // CHUNK
    k_chunk = np.arange(BAND_A)[None, :] // CHUNK
    in_band = (k_chunk >= q_chunk) & (k_chunk <= q_chunk + N_LEFT_CHUNKS)
    table = jnp.where(in_band[None], toeplitz, MASK_VALUE)
    return table.reshape(h // 2, 2 * UNIT_A, BAND_A)


def _mixer_b_kernel(q_ref, kt_ref, v_ref, cum_ref, o_ref, q_scr, acc_ref, carry_ref):
    i = pl.program_id(2)
    low_lanes = _low_lanes()
    q = q_ref[0].astype(F32) * (1.0 / math.sqrt(HEAD_DIM))
    q_scr[0:TQ_B, :] = jnp.where(low_lanes, q, 0.0).astype(BF16)
    q_scr[TQ_B:2 * TQ_B, :] = jnp.where(low_lanes, 0.0, q).astype(BF16)
    acc_ref[...] = jnp.zeros_like(acc_ref)
    carry_ref[...] = jnp.zeros_like(carry_ref)

    def visit(j, diagonal):
        v_blk = v_ref[0, pl.ds(pl.multiple_of(j * KB_B, KB_B), KB_B), :]
        neg_z = _dot(q_scr[...], kt_ref[0, 0, j]) * (-LOG2E)
        soft = jnp.log2(1.0 + jnp.exp2(-jnp.abs(neg_z)))
        log_keep = jnp.minimum(neg_z, 0.0) - soft
        log_beta = log_keep - neg_z
        if diagonal:
            row = lax.broadcasted_iota(jnp.int32, (2 * TQ_B, KB_B), 0)
            col = lax.broadcasted_iota(jnp.int32, (2 * TQ_B, KB_B), 1)
            before = col < jnp.where(row >= TQ_B, row - TQ_B, row)
            log_keep = jnp.where(before, log_keep, 0.0)
        hi = log_keep.astype(BF16)
        lo = (log_keep - hi.astype(F32)).astype(BF16)
        tail = _dot(jnp.concatenate([hi, lo], axis=1), cum_ref[...])
        block_total = tail[:, 0:1] + log_keep[:, 0:1]
        carry = carry_ref[...]
        w = jnp.exp2(log_beta + tail + jnp.concatenate([carry] * (KB_B // LANES), axis=1))
        if diagonal:
            w = jnp.where(before, w, 0.0)
        acc_ref[...] += _dot(w.astype(BF16), v_blk)
        carry_ref[...] = carry + block_total

    visit(i, True)

    def body(step, _):
        visit(i - 1 - step, False)
        return 0

    lax.fori_loop(0, i, body, 0)
    o_ref[0] = jnp.where(low_lanes, acc_ref[0:TQ_B, :], acc_ref[TQ_B:2 * TQ_B, :]).astype(BF16)


def _cumulative_matrix():
    j = np.arange(KB_B)[:, None]
    s = np.arange(KB_B)[None, :]
    strict = (j > s).astype(np.float32)
    return jnp.asarray(np.concatenate([strict, strict], axis=0), BF16)


def _mixer_b(qkvu3):
    b, s, _ = qkvu3.shape
    nb = s // KB_B
    k_cols = qkvu3[:, :, 4 * WIDTH:5 * WIDTH]
    kt = jnp.transpose(k_cols.reshape(b, nb, KB_B, N_PAIRS, LANES), (0, 3, 1, 4, 2))
    return pl.pallas_call(
        _mixer_b_kernel,
        grid=(b, N_PAIRS, s // TQ_B),
        in_specs=[
            pl.BlockSpec((1, TQ_B, LANES), lambda bi, p, i: (bi, i, 3 * N_PAIRS + p)),
            pl.BlockSpec((1, 1, nb, LANES, KB_B), lambda bi, p, i: (bi, p, 0, 0, 0)),
            pl.BlockSpec((1, s, LANES), lambda bi, p, i: (bi, 0, 5 * N_PAIRS + p)),
            _resident((2 * KB_B, KB_B)),
        ],
        out_specs=pl.BlockSpec((1, TQ_B, LANES), lambda bi, p, i: (bi, i, p)),
        out_shape=jax.ShapeDtypeStruct((b, s, WIDTH), BF16),
        scratch_shapes=[
            pltpu.VMEM((2 * TQ_B, LANES), BF16),
            pltpu.VMEM((2 * TQ_B, LANES), F32),
            pltpu.VMEM((2 * TQ_B, LANES), F32),
        ],
        compiler_params=_compiler_params(("arbitrary", "arbitrary", "arbitrary")),
        name="mixer_b",
    )(qkvu3, kt, qkvu3, _cumulative_matrix())


def _merge_kernel(tiles_per_seq, x_ref, oa_ref, ob_ref, u_ref, uh_ref, gate_ref,
                  wp_ref, ps_ref, wa_ref, wb_ref, wc_ref, wo_ref, o_ref):
    first = (pl.program_id(0) % tiles_per_seq) == 0
    u = u_ref[...].astype(F32)
    halo = jnp.where(first, 0.0, uh_ref[...].astype(F32))
    ext = jnp.concatenate([halo, u], axis=0)
    pos = lax.broadcasted_iota(jnp.int32, (TM_PROJ, 1), 0)

    pooled_groups = []
    for g, win in enumerate(POOL_WINDOWS):
        cols = slice(g * POOL_GROUP_DIM, (g + 1) * POOL_GROUP_DIM)
        a = ext[:, cols]
        step = 1
        while step < win:
            a = a + pltpu.roll(a, step, 0)
            step *= 2
        win_sum = a[HALO_POOL:]
        count = jnp.where(first, jnp.minimum(pos + 1, win), win).astype(F32)
        pooled = win_sum / count - u[:, cols]
        mixed = _dot(pooled.astype(BF16), wp_ref[g])
        pooled_groups.append(mixed * ps_ref[:, cols])
    o_c = jnp.concatenate(pooled_groups, axis=1).astype(BF16)

    merged = (gate_ref[:, 0:D_MODEL].astype(F32) * _dot(oa_ref[...], wa_ref[...])
              + gate_ref[:, D_MODEL:2 * D_MODEL].astype(F32) * _dot(ob_ref[...], wb_ref[...])
              + gate_ref[:, 2 * D_MODEL:3 * D_MODEL].astype(F32) * _dot(o_c, wc_ref[...]))
    o_ref[...] = x_ref[...] + _dot(merged.astype(BF16), wo_ref[...])


def _merge(x2, o_a, o_b, qkvu, gates, w_pool, pool_scale, w_a, w_b, w_c, w_out, seq):
    t = x2.shape[0]
    u_block = QKVU_COLS // WIDTH - 1
    halo_blocks = TM_PROJ // HALO_POOL
    return pl.pallas_call(
        functools.partial(_merge_kernel, seq // TM_PROJ),
        grid=(t // TM_PROJ,),
        in_specs=[
            pl.BlockSpec((TM_PROJ, D_MODEL), lambda i: (i, 0)),
            pl.BlockSpec((TM_PROJ, WIDTH), lambda i: (i, 0)),
            pl.BlockSpec((TM_PROJ, WIDTH), lambda i: (i, 0)),
            pl.BlockSpec((TM_PROJ, WIDTH), lambda i: (i, u_block)),
            pl.BlockSpec((HALO_POOL, WIDTH), lambda i: (jnp.maximum(i * halo_blocks - 1, 0), u_block)),
            pl.BlockSpec((TM_PROJ, GATE_COLS), lambda i: (i, 0)),
            _resident((len(POOL_WINDOWS), POOL_GROUP_DIM, POOL_GROUP_DIM)),
            _resident((1, WIDTH)),
            _resident((WIDTH, D_MODEL)),
            _resident((WIDTH, D_MODEL)),
            _resident((WIDTH, D_MODEL)),
            _resident((D_MODEL, D_MODEL)),
        ],
        out_specs=pl.BlockSpec((TM_PROJ, D_MODEL), lambda i: (i, 0)),
        out_shape=jax.ShapeDtypeStruct((t, D_MODEL), F32),
        compiler_params=_compiler_params(("arbitrary",)),
        name="merge",
    )(x2, o_a, o_b, qkvu, qkvu, gates, w_pool, pool_scale, w_a, w_b, w_c, w_out)


def _ffn_kernel(tiles_per_seq, x_ref, g_ref, wu_ref, cw_ref, cb_ref, wd_ref, o_ref, halo_ref):
    first = (pl.program_id(0) % tiles_per_seq) == 0
    x = x_ref[...]
    h = _rms_norm_rows(x, g_ref[...]).astype(BF16)
    row = lax.broadcasted_iota(jnp.int32, (TM_PROJ, 1), 0)

    def conv(cols):
        u = _dot(h, wu_ref[:, cols])
        prev = jnp.where(first, 0.0, halo_ref[:, cols])
        halo_ref[:, cols] = u[TM_PROJ - HALO_CONV:]
        back1 = prev[HALO_CONV - 1:HALO_CONV]
        back2 = prev[HALO_CONV - 2:HALO_CONV - 1]
        u1 = jnp.where(row == 0, back1, pltpu.roll(u, 1, 0))
        u2 = jnp.where(row == 0, back2, jnp.where(row == 1, back1, pltpu.roll(u, 2, 0)))
        return (cb_ref[:, cols] + cw_ref[0:1, cols] * u2 + cw_ref[1:2, cols] * u1
                + cw_ref[2:3, cols] * u)

    acc = x
    for n0 in range(0, D_FF, FF_CHUNK):
        gate = conv(slice(n0, n0 + FF_CHUNK))
        val = conv(slice(D_FF + n0, D_FF + n0 + FF_CHUNK))
        act = (jax.nn.silu(gate) * val).astype(BF16)
        acc = acc + _dot(act, wd_ref[n0:n0 + FF_CHUNK, :])
    o_ref[...] = acc


def _ffn(x2, gain, w_up, conv_w, conv_b, w_down, seq):
    t = x2.shape[0]
    return pl.pallas_call(
        functools.partial(_ffn_kernel, seq // TM_PROJ),
        grid=(t // TM_PROJ,),
        in_specs=[
            pl.BlockSpec((TM_PROJ, D_MODEL), lambda i: (i, 0)),
            _resident((1, D_MODEL)),
            _resident((D_MODEL, 2 * D_FF)),
            _resident((3, 2 * D_FF)),
            _resident((1, 2 * D_FF)),
            _resident((D_FF, D_MODEL)),
        ],
        out_specs=pl.BlockSpec((TM_PROJ, D_MODEL), lambda i: (i, 0)),
        out_shape=jax.ShapeDtypeStruct((t, D_MODEL), F32),
        scratch_shapes=[pltpu.VMEM((HALO_CONV, 2 * D_FF), F32)],
        compiler_params=_compiler_params(("arbitrary",)),
        name="ffn",
    )(x2, gain, w_up, conv_w, conv_b, w_down)


def kernel(x, norm_mix, w_in, b_gate, q_norm_a, k_norm_a, rel_bias_a, w_pool, pool_scale,
           w_branch_a, w_branch_b, w_branch_c, w_out, norm_ffn, w_up, conv_w, conv_b, w_down):
    b, s, d = x.shape
    assert d == D_MODEL and s % TQ_A == 0 and s % TM_PROJ == 0 and s % TQ_B == 0
    depth = w_in.shape[0]
    x2 = x.reshape(b * s, d)
    for l in range(depth):
        qkvu, gates = _in_proj(x2, norm_mix[l][None, :], w_in[l].astype(BF16), b_gate[l][None, :])
        qkvu3 = qkvu.reshape(b, s, QKVU_COLS)
        gq = jnp.tile(q_norm_a[l], 2)[None, :]
        gk = jnp.tile(k_norm_a[l], 2)[None, :]
        o_a = _mixer_a(qkvu3, gq, gk, _rel_bias_units(rel_bias_a[l]))
        o_b = _mixer_b(qkvu3)
        x2 = _merge(x2, o_a.reshape(b * s, WIDTH), o_b.reshape(b * s, WIDTH), qkvu, gates,
                    w_pool[l].astype(BF16), pool_scale[l][None, :],
                    w_branch_a[l].astype(BF16), w_branch_b[l].astype(BF16),
                    w_branch_c[l].astype(BF16), w_out[l].astype(BF16), s)
        x2 = _ffn(x2, norm_ffn[l][None, :], w_up[l].astype(BF16), conv_w[l], conv_b[l][None, :],
                  w_down[l].astype(BF16), s)
    return x2.reshape(b, s, d)
```

```python
import functools
import math

import jax
import jax.numpy as jnp
import numpy as np
from jax import lax
from jax.experimental import pallas as pl
from jax.experimental.pallas import tpu as pltpu

D_MODEL = 1024
CHUNK = 64
N_LEFT_CHUNKS = 8
HEAD_DIM = 64
N_HEADS = 8
N_PAIRS = N_HEADS // 2
WIDTH = N_HEADS * HEAD_DIM
POOL_WINDOWS = (2, 4, 8, 16)
POOL_GROUP_DIM = 128
MAX_REL_DIST = 2 * CHUNK
D_FF = 2816
EPS = 1e-6
MASK_VALUE = -1e30

QKVU_COLS = 7 * WIDTH
GATE_COLS = 3 * D_MODEL

LANES = 128
VMEM_LIMIT_BYTES = 56 * 1024 * 1024

TM_PROJ = 512
TN_PROJ = 512
TQ_A = N_LEFT_CHUNKS * CHUNK
UNIT_A = 2 * CHUNK
BAND_A = (N_LEFT_CHUNKS + 2) * CHUNK
TQ_B = 256
KB_B = 256
FF_CHUNK = 256
HALO_POOL = 16
HALO_CONV = 8

LOG2E = 1.4426950408889634
CARRY_FLOOR_B = -152.0

BF16 = jnp.bfloat16
F32 = jnp.float32


def _dot(a, b):
    return jnp.dot(a, b, preferred_element_type=F32)


def _rms_norm_rows(x, gain):
    ms = jnp.mean(x * x, axis=-1, keepdims=True)
    return x * lax.rsqrt(ms + EPS) * gain


def _compiler_params(semantics):
    return pltpu.CompilerParams(dimension_semantics=semantics,
                                vmem_limit_bytes=VMEM_LIMIT_BYTES)


def _resident(shape):
    zeros = (0,) * len(shape)
    return pl.BlockSpec(shape, lambda *_: zeros, pipeline_mode=pl.Buffered(1))


def _low_lanes():
    return lax.broadcasted_iota(jnp.int32, (1, LANES), 1) < HEAD_DIM


def _in_proj_kernel(x_ref, g_ref, w_ref, b_ref, qkvu_ref, gate_ref):
    h = _rms_norm_rows(x_ref[...], g_ref[...]).astype(BF16)
    for n0 in range(0, QKVU_COLS, TN_PROJ):
        qkvu_ref[:, n0:n0 + TN_PROJ] = _dot(h, w_ref[:, n0:n0 + TN_PROJ]).astype(BF16)
    for n0 in range(0, GATE_COLS, TN_PROJ):
        z = _dot(h, w_ref[:, QKVU_COLS + n0:QKVU_COLS + n0 + TN_PROJ]) + b_ref[:, n0:n0 + TN_PROJ]
        gate_ref[:, n0:n0 + TN_PROJ] = jax.nn.sigmoid(z).astype(BF16)


def _in_proj(x2, gain, w_in, b_gate):
    t = x2.shape[0]
    return pl.pallas_call(
        _in_proj_kernel,
        grid=(t // TM_PROJ,),
        in_specs=[
            pl.BlockSpec((TM_PROJ, D_MODEL), lambda i: (i, 0)),
            _resident((1, D_MODEL)),
            _resident((D_MODEL, QKVU_COLS + GATE_COLS)),
            _resident((1, GATE_COLS)),
        ],
        out_specs=[
            pl.BlockSpec((TM_PROJ, QKVU_COLS), lambda i: (i, 0)),
            pl.BlockSpec((TM_PROJ, GATE_COLS), lambda i: (i, 0)),
        ],
        out_shape=[
            jax.ShapeDtypeStruct((t, QKVU_COLS), BF16),
            jax.ShapeDtypeStruct((t, GATE_COLS), BF16),
        ],
        compiler_params=_compiler_params(("arbitrary",)),
        name="in_proj",
    )(x2, gain, w_in, b_gate)


def _head_pair_norm(x, gain, low_lanes):
    sq = x * x
    s_low = jnp.sum(jnp.where(low_lanes, sq, 0.0), axis=-1, keepdims=True)
    s_high = jnp.sum(jnp.where(low_lanes, 0.0, sq), axis=-1, keepdims=True)
    ms = jnp.where(low_lanes, s_low, s_high) * (1.0 / HEAD_DIM)
    return x * lax.rsqrt(ms + EPS) * gain


def _mixer_a_kernel(q_ref, kp_ref, kc_ref, vp_ref, vc_ref, gq_ref, gk_ref, bias_ref,
                    o_ref, q_scr, kt_scr, v_scr):
    t = pl.program_id(2)
    low_lanes = _low_lanes()
    n_units = TQ_A // UNIT_A

    q = _head_pair_norm(q_ref[0].astype(F32), gq_ref[...], low_lanes) * (1.0 / math.sqrt(HEAD_DIM))
    for u in range(n_units):
        qu = q[u * UNIT_A:(u + 1) * UNIT_A]
        q_scr[u, 0:UNIT_A, :] = jnp.where(low_lanes, qu, 0.0).astype(BF16)
        q_scr[u, UNIT_A:2 * UNIT_A, :] = jnp.where(low_lanes, 0.0, qu).astype(BF16)

    for half, k_in in enumerate((kp_ref, kc_ref)):
        kn = _head_pair_norm(k_in[0].astype(F32), gk_ref[...], low_lanes)
        kt_scr[:, half * TQ_A:(half + 1) * TQ_A] = kn.T.astype(BF16)
    v_scr[0:TQ_A, :] = vp_ref[0]
    v_scr[TQ_A:2 * TQ_A, :] = vc_ref[0]

    key_idx = lax.broadcasted_iota(jnp.int32, (1, BAND_A), 1)
    for u in range(n_units):
        k0 = u * UNIT_A
        before_start = jnp.logical_and(key_idx + k0 < TQ_A, t == 0)
        start_mask = jnp.where(before_start, MASK_VALUE, 0.0)
        logits = _dot(q_scr[u], kt_scr[:, k0:k0 + BAND_A]) + bias_ref[0] + start_mask
        m = jnp.max(logits, axis=-1, keepdims=True)
        p = jnp.exp(logits - m)
        denom = jnp.sum(p, axis=-1, keepdims=True)
        pv = _dot(p.astype(BF16), v_scr[k0:k0 + BAND_A, :]) / denom
        o_ref[0, k0:k0 + UNIT_A, :] = jnp.where(low_lanes, pv[0:UNIT_A], pv[UNIT_A:]).astype(BF16)


def _mixer_a(qkvu3, gq, gk, bias):
    b, s, _ = qkvu3.shape
    blk = (1, TQ_A, LANES)
    prev = lambda t: jnp.maximum(t - 1, 0)
    return pl.pallas_call(
        _mixer_a_kernel,
        grid=(b, N_PAIRS, s // TQ_A),
        in_specs=[
            pl.BlockSpec(blk, lambda bi, p, t: (bi, t, p)),
            pl.BlockSpec(blk, lambda bi, p, t: (bi, prev(t), N_PAIRS + p)),
            pl.BlockSpec(blk, lambda bi, p, t: (bi, t, N_PAIRS + p)),
            pl.BlockSpec(blk, lambda bi, p, t: (bi, prev(t), 2 * N_PAIRS + p)),
            pl.BlockSpec(blk, lambda bi, p, t: (bi, t, 2 * N_PAIRS + p)),
            _resident((1, LANES)),
            _resident((1, LANES)),
            pl.BlockSpec((1, 2 * UNIT_A, BAND_A), lambda bi, p, t: (p, 0, 0)),
        ],
        out_specs=pl.BlockSpec(blk, lambda bi, p, t: (bi, t, p)),
        out_shape=jax.ShapeDtypeStruct((b, s, WIDTH), BF16),
        scratch_shapes=[
            pltpu.VMEM((TQ_A // UNIT_A, 2 * UNIT_A, LANES), BF16),
            pltpu.VMEM((LANES, 2 * TQ_A), BF16),
            pltpu.VMEM((2 * TQ_A, LANES), BF16),
        ],
        compiler_params=_compiler_params(("arbitrary", "arbitrary", "arbitrary")),
        name="mixer_a",
    )(qkvu3, qkvu3, qkvu3, qkvu3, qkvu3, gq, gk, bias)


def _rel_bias_units(rel_bias):
    period = UNIT_A + BAND_A
    d = np.arange(period)
    d = np.where(d < BAND_A, d, d - period)
    idx = np.clip(N_LEFT_CHUNKS * CHUNK - d, -(CHUNK - 1), MAX_REL_DIST) + (CHUNK - 1)
    vec = rel_bias.astype(F32)[:, idx]
    h = rel_bias.shape[0]
    rows = jnp.tile(vec, (1, UNIT_A))[:, :UNIT_A * (period - 1)]
    toeplitz = rows.reshape(h, UNIT_A, period - 1)[:, :, :BAND_A]
    q_chunk = np.arange(UNIT_A)[:, None] // CHUNK
    k_chunk = np.arange(BAND_A)[None, :] // CHUNK
    in_band = (k_chunk >= q_chunk) & (k_chunk <= q_chunk + N_LEFT_CHUNKS)
    table = jnp.where(in_band[None], toeplitz, MASK_VALUE)
    return table.reshape(h // 2, 2 * UNIT_A, BAND_A)


def _mixer_b_kernel(q_ref, kt_ref, v_ref, cum_ref, o_ref, q_scr, acc_ref, carry_ref):
    i = pl.program_id(2)
    low_lanes = _low_lanes()
    q = q_ref[0].astype(F32) * (1.0 / math.sqrt(HEAD_DIM))
    q_scr[0:TQ_B, :] = jnp.where(low_lanes, q, 0.0).astype(BF16)
    q_scr[TQ_B:2 * TQ_B, :] = jnp.where(low_lanes, 0.0, q).astype(BF16)
    acc_ref[...] = jnp.zeros_like(acc_ref)
    carry_ref[...] = jnp.zeros_like(carry_ref)

    def visit(j, diagonal):
        v_blk = v_ref[0, pl.ds(pl.multiple_of(j * KB_B, KB_B), KB_B), :]
        neg_z = _dot(q_scr[...], kt_ref[0, 0, j]) * (-LOG2E)
        soft = jnp.log2(1.0 + jnp.exp2(-jnp.abs(neg_z)))
        log_keep = jnp.minimum(neg_z, 0.0) - soft
        log_beta = log_keep - neg_z
        if diagonal:
            row = lax.broadcasted_iota(jnp.int32, (2 * TQ_B, KB_B), 0)
            col = lax.broadcasted_iota(jnp.int32, (2 * TQ_B, KB_B), 1)
            before = col < jnp.where(row >= TQ_B, row - TQ_B, row)
            log_keep = jnp.where(before, log_keep, 0.0)
        hi = log_keep.astype(BF16)
        lo = (log_keep - hi.astype(F32)).astype(BF16)
        tail = _dot(jnp.concatenate([hi, lo], axis=1), cum_ref[...])
        block_total = tail[:, 0:1] + log_keep[:, 0:1]
        carry = carry_ref[...]
        w = jnp.exp2(log_beta + tail + jnp.concatenate([carry] * (KB_B // LANES), axis=1))
        if diagonal:
            w = jnp.where(before, w, 0.0)
        acc_ref[...] += _dot(w.astype(BF16), v_blk)
        carry_ref[...] = carry + block_total

    visit(i, True)

    def more_keys(state):
        step, live = state
        return jnp.logical_and(step < i, live > 0)

    def body(state):
        step, _ = state
        visit(i - 1 - step, False)
        live = (jnp.max(carry_ref[...]) >= CARRY_FLOOR_B).astype(jnp.int32)
        return step + 1, live

    lax.while_loop(more_keys, body, (jnp.int32(0), jnp.int32(1)))
    o_ref[0] =jnp.where(low_lanes, acc_ref[0:TQ_B, :], acc_ref[TQ_B:2 * TQ_B, :]).astype(BF16)


def _cumulative_matrix():
    j = np.arange(KB_B)[:, None]
    s = np.arange(KB_B)[None, :]
    strict = (j > s).astype(np.float32)
    return jnp.asarray(np.concatenate([strict, strict], axis=0), BF16)


def _mixer_b(qkvu3):
    b, s, _ = qkvu3.shape
    nb = s // KB_B
    k_cols = qkvu3[:, :, 4 * WIDTH:5 * WIDTH]
    kt = jnp.transpose(k_cols.reshape(b, nb, KB_B, N_PAIRS, LANES), (0, 3, 1, 4, 2))
    return pl.pallas_call(
        _mixer_b_kernel,
        grid=(b, N_PAIRS, s // TQ_B),
        in_specs=[
            pl.BlockSpec((1, TQ_B, LANES), lambda bi, p, i: (bi, i, 3 * N_PAIRS + p)),
            pl.BlockSpec((1, 1, nb, LANES, KB_B), lambda bi, p, i: (bi, p, 0, 0, 0)),
            pl.BlockSpec((1, s, LANES), lambda bi, p, i: (bi, 0, 5 * N_PAIRS + p)),
            _resident((2 * KB_B, KB_B)),
        ],
        out_specs=pl.BlockSpec((1, TQ_B, LANES), lambda bi, p, i: (bi, i, p)),
        out_shape=jax.ShapeDtypeStruct((b, s, WIDTH), BF16),
        scratch_shapes=[
            pltpu.VMEM((2 * TQ_B, LANES), BF16),
            pltpu.VMEM((2 * TQ_B, LANES), F32),
            pltpu.VMEM((2 * TQ_B, LANES), F32),
        ],
        compiler_params=_compiler_params(("arbitrary", "arbitrary", "arbitrary")),
        name="mixer_b",
    )(qkvu3, kt, qkvu3, _cumulative_matrix())


def _merge_kernel(tiles_per_seq, x_ref, oa_ref, ob_ref, u_ref, uh_ref, gate_ref,
                  wp_ref, ps_ref, wa_ref, wb_ref, wc_ref, wo_ref, o_ref):
    first = (pl.program_id(0) % tiles_per_seq) == 0
    u = u_ref[...].astype(F32)
    halo = jnp.where(first, 0.0, uh_ref[...].astype(F32))
    ext = jnp.concatenate([halo, u], axis=0)
    pos = lax.broadcasted_iota(jnp.int32, (TM_PROJ, 1), 0)

    pooled_groups = []
    for g, win in enumerate(POOL_WINDOWS):
        cols = slice(g * POOL_GROUP_DIM, (g + 1) * POOL_GROUP_DIM)
        a = ext[:, cols]
        step = 1
        while step < win:
            a = a + pltpu.roll(a, step, 0)
            step *= 2
        win_sum = a[HALO_POOL:]
        count = jnp.where(first, jnp.minimum(pos + 1, win), win).astype(F32)
        pooled = win_sum / count - u[:, cols]
        mixed = _dot(pooled.astype(BF16), wp_ref[g])
        pooled_groups.append(mixed * ps_ref[:, cols])
    o_c = jnp.concatenate(pooled_groups, axis=1).astype(BF16)

    merged = (gate_ref[:, 0:D_MODEL].astype(F32) * _dot(oa_ref[...], wa_ref[...])
              + gate_ref[:, D_MODEL:2 * D_MODEL].astype(F32) * _dot(ob_ref[...], wb_ref[...])
              + gate_ref[:, 2 * D_MODEL:3 * D_MODEL].astype(F32) * _dot(o_c, wc_ref[...]))
    o_ref[...] = x_ref[...] + _dot(merged.astype(BF16), wo_ref[...])


def _merge(x2, o_a, o_b, qkvu, gates, w_pool, pool_scale, w_a, w_b, w_c, w_out, seq):
    t = x2.shape[0]
    u_block = QKVU_COLS // WIDTH - 1
    halo_blocks = TM_PROJ // HALO_POOL
    return pl.pallas_call(
        functools.partial(_merge_kernel, seq // TM_PROJ),
        grid=(t // TM_PROJ,),
        in_specs=[
            pl.BlockSpec((TM_PROJ, D_MODEL), lambda i: (i, 0)),
            pl.BlockSpec((TM_PROJ, WIDTH), lambda i: (i, 0)),
            pl.BlockSpec((TM_PROJ, WIDTH), lambda i: (i, 0)),
            pl.BlockSpec((TM_PROJ, WIDTH), lambda i: (i, u_block)),
            pl.BlockSpec((HALO_POOL, WIDTH), lambda i: (jnp.maximum(i * halo_blocks - 1, 0), u_block)),
            pl.BlockSpec((TM_PROJ, GATE_COLS), lambda i: (i, 0)),
            _resident((len(POOL_WINDOWS), POOL_GROUP_DIM, POOL_GROUP_DIM)),
            _resident((1, WIDTH)),
            _resident((WIDTH, D_MODEL)),
            _resident((WIDTH, D_MODEL)),
            _resident((WIDTH, D_MODEL)),
            _resident((D_MODEL, D_MODEL)),
        ],
        out_specs=pl.BlockSpec((TM_PROJ, D_MODEL), lambda i: (i, 0)),
        out_shape=jax.ShapeDtypeStruct((t, D_MODEL), F32),
        compiler_params=_compiler_params(("arbitrary",)),
        name="merge",
    )(x2, o_a, o_b, qkvu, qkvu, gates, w_pool, pool_scale, w_a, w_b, w_c, w_out)


def _ffn_kernel(tiles_per_seq, x_ref, g_ref, wu_ref, cw_ref, cb_ref, wd_ref, o_ref, halo_ref, h_ref):
    first = (pl.program_id(0) % tiles_per_seq) == 0
    x = x_ref[...]
    h_ref[...] = _rms_norm_rows(x, g_ref[...]).astype(BF16)
    o_ref[...] = x
    row = lax.broadcasted_iota(jnp.int32, (HALO_CONV, 1), 0)

    def up(n0):
        return tuple(_dot(h_ref[...], wu_ref[:, c0:c0 + FF_CHUNK]) for c0 in (n0, D_FF + n0))

    def conv(u, c0):
        cols = slice(c0, c0 + FF_CHUNK)
        prev = jnp.where(first, 0.0, halo_ref[:, cols])
        halo_ref[:, cols] = u[TM_PROJ - HALO_CONV:]
        back1 = prev[HALO_CONV - 1:HALO_CONV]
        back2 = prev[HALO_CONV - 2:HALO_CONV - 1]
        r1 = pltpu.roll(u, 1, 0)
        r2 = pltpu.roll(u, 2, 0)
        top1 = jnp.where(row == 0, back1, r1[0:HALO_CONV])
        top2 = jnp.where(row == 0, back2, jnp.where(row == 1, back1, r2[0:HALO_CONV]))
        u1 = jnp.concatenate([top1, r1[HALO_CONV:]], axis=0)
        u2 = jnp.concatenate([top2, r2[HALO_CONV:]], axis=0)
        return (cb_ref[:, cols] + cw_ref[0:1, cols] * u2 + cw_ref[1:2, cols] * u1
                + cw_ref[2:3, cols] * u)

    chunks = list(range(0, D_FF, FF_CHUNK))
    u_next = up(chunks[0])
    for k, n0 in enumerate(chunks):
        u_gate, u_val = u_next
        if k + 1 < len(chunks):
            u_next = up(chunks[k + 1])
        act = (jax.nn.silu(conv(u_gate, n0)) * conv(u_val, D_FF + n0)).astype(BF16)
        o_ref[...] += _dot(act, wd_ref[n0:n0 + FF_CHUNK, :])


def _ffn(x2, gain, w_up, conv_w, conv_b, w_down, seq):
    t = x2.shape[0]
    return pl.pallas_call(
        functools.partial(_ffn_kernel, seq // TM_PROJ),
        grid=(t // TM_PROJ,),
        in_specs=[
            pl.BlockSpec((TM_PROJ, D_MODEL), lambda i: (i, 0)),
            _resident((1, D_MODEL)),
            _resident((D_MODEL, 2 * D_FF)),
            _resident((3, 2 * D_FF)),
            _resident((1, 2 * D_FF)),
            _resident((D_FF, D_MODEL)),
        ],
        out_specs=pl.BlockSpec((TM_PROJ, D_MODEL), lambda i: (i, 0)),
        out_shape=jax.ShapeDtypeStruct((t, D_MODEL), F32),
        scratch_shapes=[pltpu.VMEM((HALO_CONV, 2 * D_FF), F32), pltpu.VMEM((TM_PROJ, D_MODEL), BF16)],
        compiler_params=_compiler_params(("arbitrary",)),
        name="ffn",
    )(x2, gain, w_up, conv_w, conv_b, w_down)


def kernel(x, norm_mix, w_in, b_gate, q_norm_a, k_norm_a, rel_bias_a, w_pool, pool_scale,
           w_branch_a, w_branch_b, w_branch_c, w_out, norm_ffn, w_up, conv_w, conv_b, w_down):
    b, s, d = x.shape
    assert d == D_MODEL and s % TQ_A == 0 and s % TM_PROJ == 0 and s % TQ_B == 0
    depth = w_in.shape[0]
    x2 = x.reshape(b * s, d)
    for l in range(depth):
        qkvu, gates = _in_proj(x2, norm_mix[l][None, :], w_in[l].astype(BF16), b_gate[l][None, :])
        qkvu3 = qkvu.reshape(b, s, QKVU_COLS)
        gq = jnp.tile(q_norm_a[l], 2)[None, :]
        gk = jnp.tile(k_norm_a[l], 2)[None, :]
        o_a = _mixer_a(qkvu3, gq, gk, _rel_bias_units(rel_bias_a[l]))
        o_b = _mixer_b(qkvu3)
        x2 = _merge(x2, o_a.reshape(b * s, WIDTH), o_b.reshape(b * s, WIDTH), qkvu, gates,
                    w_pool[l].astype(BF16), pool_scale[l][None, :],
                    w_branch_a[l].astype(BF16), w_branch_b[l].astype(BF16),
                    w_branch_c[l].astype(BF16), w_out[l].astype(BF16), s)
        x2 = _ffn(x2, norm_ffn[l][None, :], w_up[l].astype(BF16), conv_w[l], conv_b[l][None, :],
                  w_down[l].astype(BF16), s)
    return x2.reshape(b, s, d)
```

```python
import functools
import math

import jax
import jax.numpy as jnp
import numpy as np
from jax import lax
from jax.experimental import pallas as pl
from jax.experimental.pallas import tpu as pltpu

D_MODEL = 1024
CHUNK = 64
N_LEFT_CHUNKS = 8
HEAD_DIM = 64
N_HEADS = 8
N_PAIRS = N_HEADS // 2
WIDTH = N_HEADS * HEAD_DIM
POOL_WINDOWS = (2, 4, 8, 16)
POOL_GROUP_DIM = 128
MAX_REL_DIST = 2 * CHUNK
D_FF = 2816
EPS = 1e-6
MASK_VALUE = -1e30

QKVU_COLS = 7 * WIDTH
GATE_COLS = 3 * D_MODEL

LANES = 128
VMEM_LIMIT_BYTES = 56 * 1024 * 1024

TM_PROJ = 512
TN_PROJ = 512
TM_FFN = 512
TQ_A = N_LEFT_CHUNKS * CHUNK
UNIT_A = 2 * CHUNK
BAND_A = (N_LEFT_CHUNKS + 2) * CHUNK
TQ_B = 256
KB_B = 256
FF_CHUNK = 256
HALO_POOL = 16
HALO_CONV = 8

LOG2E = 1.4426950408889634
CARRY_FLOOR_B = -152.0

BF16 = jnp.bfloat16
F32 = jnp.float32


def _dot(a, b):
    return jnp.dot(a, b, preferred_element_type=F32)


def _dot_nt(a, b):
    return lax.dot_general(a, b, (((1,), (1,)), ((), ())), preferred_element_type=F32)


def _rms_norm_rows(x, gain):
    ms = jnp.mean(x * x, axis=-1, keepdims=True)
    return x * lax.rsqrt(ms + EPS) * gain


def _compiler_params(semantics):
    return pltpu.CompilerParams(dimension_semantics=semantics,
                                vmem_limit_bytes=VMEM_LIMIT_BYTES)


def _resident(shape):
    zeros = (0,) * len(shape)
    return pl.BlockSpec(shape, lambda *_: zeros, pipeline_mode=pl.Buffered(1))


def _layer_resident(layer, shape):
    index = (layer,) + (0,) * len(shape)
    return pl.BlockSpec((None,) + tuple(shape), lambda *_: index, pipeline_mode=pl.Buffered(1))


def _low_lanes():
    return lax.broadcasted_iota(jnp.int32, (1, LANES), 1) < HEAD_DIM


def _in_proj_kernel(x_ref, g_ref, w_ref, b_ref, qkvu_ref, gate_ref):
    h = _rms_norm_rows(x_ref[...], g_ref[...]).astype(BF16)
    for n0 in range(0, QKVU_COLS, TN_PROJ):
        qkvu_ref[:, n0:n0 + TN_PROJ] = _dot(h, w_ref[:, n0:n0 + TN_PROJ]).astype(BF16)
    for n0 in range(0, GATE_COLS, TN_PROJ):
        z = _dot(h, w_ref[:, QKVU_COLS + n0:QKVU_COLS + n0 + TN_PROJ]) + b_ref[:, n0:n0 + TN_PROJ]
        gate_ref[:, n0:n0 + TN_PROJ] = jax.nn.sigmoid(z).astype(BF16)


def _in_proj(x2, gain, w_in, b_gate, layer):
    t = x2.shape[0]
    return pl.pallas_call(
        _in_proj_kernel,
        grid=(t // TM_PROJ,),
        in_specs=[
            pl.BlockSpec((TM_PROJ, D_MODEL), lambda i: (i, 0)),
            _resident((1, D_MODEL)),
            _layer_resident(layer, (D_MODEL, QKVU_COLS + GATE_COLS)),
            _resident((1, GATE_COLS)),
        ],
        out_specs=[
            pl.BlockSpec((TM_PROJ, QKVU_COLS), lambda i: (i, 0)),
            pl.BlockSpec((TM_PROJ, GATE_COLS), lambda i: (i, 0)),
        ],
        out_shape=[
            jax.ShapeDtypeStruct((t, QKVU_COLS), BF16),
            jax.ShapeDtypeStruct((t, GATE_COLS), BF16),
        ],
        compiler_params=_compiler_params(("arbitrary",)),
        name="in_proj",
    )(x2, gain, w_in, b_gate)


def _head_pair_norm(x, gain, low_lanes):
    sq = x * x
    s_low = jnp.sum(jnp.where(low_lanes, sq, 0.0), axis=-1, keepdims=True)
    s_high = jnp.sum(jnp.where(low_lanes, 0.0, sq), axis=-1, keepdims=True)
    ms = jnp.where(low_lanes, s_low, s_high) * (1.0 / HEAD_DIM)
    return x * lax.rsqrt(ms + EPS) * gain


def _mixer_a_kernel(q_ref, k_ref, v_ref, gq_ref, gk_ref, bias_ref, o_ref, q_scr, kt_scr, v_scr):
    t = pl.program_id(2)
    low_lanes = _low_lanes()
    n_units = TQ_A // UNIT_A

    q = _head_pair_norm(q_ref[0].astype(F32), gq_ref[...], low_lanes) * (1.0 / math.sqrt(HEAD_DIM))
    for u in range(n_units):
        qu = q[u * UNIT_A:(u + 1) * UNIT_A]
        q_scr[u, 0:UNIT_A, :] = jnp.where(low_lanes, qu, 0.0).astype(BF16)
        q_scr[u, UNIT_A:2 * UNIT_A, :] = jnp.where(low_lanes, 0.0, qu).astype(BF16)

    @pl.when(t == 0)
    def _():
        kt_scr[:, 0:TQ_A] = jnp.zeros((LANES, TQ_A), BF16)
        v_scr[0:TQ_A, :] = jnp.zeros((TQ_A, LANES), BF16)

    @pl.when(t > 0)
    def _():
        kt_scr[:, 0:TQ_A] = kt_scr[:, TQ_A:2 * TQ_A]
        v_scr[0:TQ_A, :] = v_scr[TQ_A:2 * TQ_A, :]

    kn = _head_pair_norm(k_ref[0].astype(F32), gk_ref[...], low_lanes)
    kt_scr[:, TQ_A:2 * TQ_A] = kn.T.astype(BF16)
    v_scr[TQ_A:2 * TQ_A, :] = v_ref[0]

    key_idx = lax.broadcasted_iota(jnp.int32, (1, BAND_A), 1)

    def scores(u):
        return _dot(q_scr[u], kt_scr[:, u * UNIT_A:u * UNIT_A + BAND_A])

    s_next = scores(0)
    for u in range(n_units):
        k0 = u * UNIT_A
        s_now = s_next
        if u + 1 < n_units:
            s_next = scores(u + 1)
        before_start = jnp.logical_and(key_idx + k0 < TQ_A, t == 0)
        start_mask = jnp.where(before_start, MASK_VALUE, 0.0)
        logits = s_now + (bias_ref[0] + start_mask)
        m = jnp.max(logits, axis=-1, keepdims=True)
        p = jnp.exp(logits - m)
        denom = jnp.sum(p, axis=-1, keepdims=True)
        pv = _dot(p.astype(BF16), v_scr[k0:k0 + BAND_A, :]) / denom
        o_ref[0, k0:k0 + UNIT_A, :] = jnp.where(low_lanes, pv[0:UNIT_A], pv[UNIT_A:]).astype(BF16)


def _mixer_a(qkvu3, gq, gk, bias):
    b, s, _ = qkvu3.shape
    blk = (1, TQ_A, LANES)
    return pl.pallas_call(
        _mixer_a_kernel,
        grid=(b, N_PAIRS, s // TQ_A),
        in_specs=[
            pl.BlockSpec(blk, lambda bi, p, t: (bi, t, p)),
            pl.BlockSpec(blk, lambda bi, p, t: (bi, t, N_PAIRS + p)),
            pl.BlockSpec(blk, lambda bi, p, t: (bi, t, 2 * N_PAIRS + p)),
            _resident((1, LANES)),
            _resident((1, LANES)),
            pl.BlockSpec((1, 2 * UNIT_A, BAND_A), lambda bi, p, t: (p, 0, 0)),
        ],
        out_specs=pl.BlockSpec(blk, lambda bi, p, t: (bi, t, p)),
        out_shape=jax.ShapeDtypeStruct((b, s, WIDTH), BF16),
        scratch_shapes=[
            pltpu.VMEM((TQ_A // UNIT_A, 2 * UNIT_A, LANES), BF16),
            pltpu.VMEM((LANES, 2 * TQ_A), BF16),
            pltpu.VMEM((2 * TQ_A, LANES), BF16),
        ],
        compiler_params=_compiler_params(("arbitrary", "arbitrary", "arbitrary")),
        name="mixer_a",
    )(qkvu3, qkvu3, qkvu3, gq, gk, bias)


def _rel_bias_units(rel_bias):
    period = UNIT_A + BAND_A
    d = np.arange(period)
    d = np.where(d < BAND_A, d, d - period)
    idx = np.clip(N_LEFT_CHUNKS * CHUNK - d, -(CHUNK - 1), MAX_REL_DIST) + (CHUNK - 1)
    vec = rel_bias.astype(F32)[:, idx]
    h = rel_bias.shape[0]
    rows = jnp.tile(vec, (1, UNIT_A))[:, :UNIT_A * (period - 1)]
    toeplitz = rows.reshape(h, UNIT_A, period - 1)[:, :, :BAND_A]
    q_chunk = np.arange(UNIT_A)[:, None] // CHUNK
    k_chunk = np.arange(BAND_A)[None, :] // CHUNK
    in_band = (k_chunk >= q_chunk) & (k_chunk <= q_chunk + N_LEFT_CHUNKS)
    table = jnp.where(in_band[None], toeplitz, MASK_VALUE)
    return table.reshape(h // 2, 2 * UNIT_A, BAND_A)


def _mixer_b_kernel(q_ref, k_ref, v_ref, cum_ref, o_ref, q_scr, acc_ref, carry_ref):
    i = pl.program_id(2)
    low_lanes = _low_lanes()
    q = q_ref[0].astype(F32) * (1.0 / math.sqrt(HEAD_DIM))
    q_scr[0:TQ_B, :] = jnp.where(low_lanes, q, 0.0).astype(BF16)
    q_scr[TQ_B:2 * TQ_B, :] = jnp.where(low_lanes, 0.0, q).astype(BF16)

    def visit(j, carry, diagonal):
        start = pl.multiple_of(j * KB_B, KB_B)
        neg_z = _dot_nt(q_scr[...], k_ref[0, pl.ds(start, KB_B), :]) * (-LOG2E)
        soft = jnp.log2(1.0 + jnp.exp2(-jnp.abs(neg_z)))
        log_keep = jnp.minimum(neg_z, 0.0) - soft
        log_beta = log_keep - neg_z
        if diagonal:
            row = lax.broadcasted_iota(jnp.int32, (2 * TQ_B, KB_B), 0)
            col = lax.broadcasted_iota(jnp.int32, (2 * TQ_B, KB_B), 1)
            before = col < jnp.where(row >= TQ_B, row - TQ_B, row)
            log_keep = jnp.where(before, log_keep, 0.0)
        hi = log_keep.astype(BF16)
        lo = (log_keep - hi.astype(F32)).astype(BF16)
        tail = _dot(jnp.concatenate([hi, lo], axis=1), cum_ref[...])
        exponent = log_beta + tail
        if carry is not None:
            exponent = exponent + jnp.concatenate([carry] * (KB_B // LANES), axis=1)
        w = jnp.exp2(exponent)
        if diagonal:
            w = jnp.where(before, w, 0.0)
        pv = _dot(w.astype(BF16), v_ref[0, pl.ds(start, KB_B), :])
        block_total = jnp.broadcast_to(tail[:, 0:1] + log_keep[:, 0:1], (2 * TQ_B, LANES))
        return pv, (block_total if carry is None else carry + block_total)

    @pl.when(i == 0)
    def _():
        pv, carry = visit(0, None, True)
        acc_ref[...] = pv
        carry_ref[...] = carry

    @pl.when(i > 0)
    def _():
        pv0, carry0 = visit(i, None, True)
        pv1, carry1 = visit(i - 1, carry0, False)
        acc_ref[...] = pv0 + pv1
        carry_ref[...] = carry1

    def is_live(carry):
        return (jnp.max(carry) >= CARRY_FLOOR_B).astype(jnp.int32)

    def more_keys(state):
        step, live = state
        return jnp.logical_and(step < i, live > 0)

    def body(state):
        step, _ = state
        pv, carry = visit(i - 1 - step, carry_ref[...], False)
        acc_ref[...] += pv
        carry_ref[...] = carry
        return step + 1, is_live(carry)

    lax.while_loop(more_keys, body, (jnp.int32(1), is_live(carry_ref[...])))
    o_ref[0] = jnp.where(low_lanes, acc_ref[0:TQ_B, :], acc_ref[TQ_B:2 * TQ_B, :]).astype(BF16)


def _cumulative_matrix():
    j = np.arange(KB_B)[:, None]
    s = np.arange(KB_B)[None, :]
    strict = (j > s).astype(np.float32)
    return jnp.asarray(np.concatenate([strict, strict], axis=0), BF16)


def _mixer_b(qkvu3):
    b, s, _ = qkvu3.shape
    return pl.pallas_call(
        _mixer_b_kernel,
        grid=(b, N_PAIRS, s // TQ_B),
        in_specs=[
            pl.BlockSpec((1, TQ_B, LANES), lambda bi, p, i: (bi, i, 3 * N_PAIRS + p)),
            pl.BlockSpec((1, s, LANES), lambda bi, p, i: (bi, 0, 4 * N_PAIRS + p)),
            pl.BlockSpec((1, s, LANES), lambda bi, p, i: (bi, 0, 5 * N_PAIRS + p)),
            _resident((2 * KB_B, KB_B)),
        ],
        out_specs=pl.BlockSpec((1, TQ_B, LANES), lambda bi, p, i: (bi, i, p)),
        out_shape=jax.ShapeDtypeStruct((b, s, WIDTH), BF16),
        scratch_shapes=[
            pltpu.VMEM((2 * TQ_B, LANES), BF16),
            pltpu.VMEM((2 * TQ_B, LANES), F32),
            pltpu.VMEM((2 * TQ_B, LANES), F32),
        ],
        compiler_params=_compiler_params(("arbitrary", "arbitrary", "arbitrary")),
        name="mixer_b",
    )(qkvu3, qkvu3, qkvu3, _cumulative_matrix())


def _merge_kernel(tiles_per_seq, x_ref, oa_ref, ob_ref, u_ref, uh_ref, gate_ref,
                  wp_ref, ps_ref, wa_ref, wb_ref, wc_ref, wo_ref, o_ref):
    first = (pl.program_id(0) % tiles_per_seq) == 0
    u = u_ref[...].astype(F32)
    halo = jnp.where(first, 0.0, uh_ref[...].astype(F32))
    ext = jnp.concatenate([halo, u], axis=0)
    pos = lax.broadcasted_iota(jnp.int32, (TM_PROJ, 1), 0)

    pooled_groups = []
    for g, win in enumerate(POOL_WINDOWS):
        cols = slice(g * POOL_GROUP_DIM, (g + 1) * POOL_GROUP_DIM)
        a = ext[:, cols]
        step = 1
        while step < win:
            a = a + pltpu.roll(a, step, 0)
            step *= 2
        win_sum = a[HALO_POOL:]
        count = jnp.where(first, jnp.minimum(pos + 1, win), win).astype(F32)
        pooled = win_sum / count - u[:, cols]
        mixed = _dot(pooled.astype(BF16), wp_ref[g])
        pooled_groups.append(mixed * ps_ref[:, cols])
    o_c = jnp.concatenate(pooled_groups, axis=1).astype(BF16)

    merged = (gate_ref[:, 0:D_MODEL].astype(F32) * _dot(oa_ref[...], wa_ref[...])
              + gate_ref[:, D_MODEL:2 * D_MODEL].astype(F32) * _dot(ob_ref[...], wb_ref[...])
              + gate_ref[:, 2 * D_MODEL:3 * D_MODEL].astype(F32) * _dot(o_c, wc_ref[...]))
    o_ref[...] = x_ref[...] + _dot(merged.astype(BF16), wo_ref[...])


def _merge(x2, o_a, o_b, qkvu, gates, w_pool, pool_scale, w_a, w_b, w_c, w_out, seq, layer):
    t = x2.shape[0]
    u_block = QKVU_COLS // WIDTH - 1
    halo_blocks = TM_PROJ // HALO_POOL
    return pl.pallas_call(
        functools.partial(_merge_kernel, seq // TM_PROJ),
        grid=(t // TM_PROJ,),
        in_specs=[
            pl.BlockSpec((TM_PROJ, D_MODEL), lambda i: (i, 0)),
            pl.BlockSpec((TM_PROJ, WIDTH), lambda i: (i, 0)),
            pl.BlockSpec((TM_PROJ, WIDTH), lambda i: (i, 0)),
            pl.BlockSpec((TM_PROJ, WIDTH), lambda i: (i, u_block)),
            pl.BlockSpec((HALO_POOL, WIDTH), lambda i: (jnp.maximum(i * halo_blocks - 1, 0), u_block)),
            pl.BlockSpec((TM_PROJ, GATE_COLS), lambda i: (i, 0)),
            _layer_resident(layer, (len(POOL_WINDOWS), POOL_GROUP_DIM, POOL_GROUP_DIM)),
            _resident((1, WIDTH)),
            _layer_resident(layer, (WIDTH, D_MODEL)),
            _layer_resident(layer, (WIDTH, D_MODEL)),
            _layer_resident(layer, (WIDTH, D_MODEL)),
            _layer_resident(layer, (D_MODEL, D_MODEL)),
        ],
        out_specs=pl.BlockSpec((TM_PROJ, D_MODEL), lambda i: (i, 0)),
        out_shape=jax.ShapeDtypeStruct((t, D_MODEL), F32),
        compiler_params=_compiler_params(("arbitrary",)),
        name="merge",
    )(x2, o_a, o_b, qkvu, qkvu, gates, w_pool, pool_scale, w_a, w_b, w_c, w_out)


def _ffn_kernel(tiles_per_seq, x_ref, g_ref, wu_ref, cw_ref, cb_ref, wd_ref, o_ref, halo_ref, h_ref):
    first = (pl.program_id(0) % tiles_per_seq) == 0
    x = x_ref[...]
    h_ref[...] = _rms_norm_rows(x, g_ref[...]).astype(BF16)
    o_ref[...] = x
    row = lax.broadcasted_iota(jnp.int32, (HALO_CONV, 1), 0)

    def up(n0):
        return tuple(_dot(h_ref[...], wu_ref[:, c0:c0 + FF_CHUNK]) for c0 in (n0, D_FF + n0))

    def conv(u, c0):
        cols = slice(c0, c0 + FF_CHUNK)
        prev = jnp.where(first, 0.0, halo_ref[:, cols])
        halo_ref[:, cols] = u[TM_FFN - HALO_CONV:]
        back1 = prev[HALO_CONV - 1:HALO_CONV]
        back2 = prev[HALO_CONV - 2:HALO_CONV - 1]
        r1 = pltpu.roll(u, 1, 0)
        r2 = pltpu.roll(u, 2, 0)
        top1 = jnp.where(row == 0, back1, r1[0:HALO_CONV])
        top2 = jnp.where(row == 0, back2, jnp.where(row == 1, back1, r2[0:HALO_CONV]))
        u1 = jnp.concatenate([top1, r1[HALO_CONV:]], axis=0)
        u2 = jnp.concatenate([top2, r2[HALO_CONV:]], axis=0)
        return (cb_ref[:, cols] + cw_ref[0:1, cols] * u2 + cw_ref[1:2, cols] * u1
                + cw_ref[2:3, cols] * u)

    chunks = list(range(0, D_FF, FF_CHUNK))
    u_next = up(chunks[0])
    for k, n0 in enumerate(chunks):
        u_gate, u_val = u_next
        if k + 1 < len(chunks):
            u_next = up(chunks[k + 1])
        act = (jax.nn.silu(conv(u_gate, n0)) * conv(u_val, D_FF + n0)).astype(BF16)
        o_ref[...] += _dot(act, wd_ref[n0:n0 + FF_CHUNK, :])


def _ffn(x2, gain, w_up, conv_w, conv_b, w_down, seq, layer):
    t = x2.shape[0]
    return pl.pallas_call(
        functools.partial(_ffn_kernel, seq // TM_FFN),
        grid=(t // TM_FFN,),
        in_specs=[
            pl.BlockSpec((TM_FFN, D_MODEL), lambda i: (i, 0)),
            _resident((1, D_MODEL)),
            _layer_resident(layer, (D_MODEL, 2 * D_FF)),
            _resident((3, 2 * D_FF)),
            _resident((1, 2 * D_FF)),
            _layer_resident(layer, (D_FF, D_MODEL)),
        ],
        out_specs=pl.BlockSpec((TM_FFN, D_MODEL), lambda i: (i, 0)),
        out_shape=jax.ShapeDtypeStruct((t, D_MODEL), F32),
        scratch_shapes=[pltpu.VMEM((HALO_CONV, 2 * D_FF), F32), pltpu.VMEM((TM_FFN, D_MODEL), BF16)],
        compiler_params=_compiler_params(("arbitrary",)),
        name="ffn",
    )(x2, gain, w_up, conv_w, conv_b, w_down)


def kernel(x, norm_mix, w_in, b_gate, q_norm_a, k_norm_a, rel_bias_a, w_pool, pool_scale,
           w_branch_a, w_branch_b, w_branch_c, w_out, norm_ffn, w_up, conv_w, conv_b, w_down):
    b, s, d = x.shape
    assert d == D_MODEL and s % TQ_A == 0 and s % TM_PROJ == 0 and s % TQ_B == 0
    depth = w_in.shape[0]
    w_in, w_pool, w_branch_a, w_branch_b, w_branch_c, w_out, w_up, w_down = (
        w.astype(BF16) for w in (w_in, w_pool, w_branch_a, w_branch_b, w_branch_c, w_out, w_up, w_down))
    x2 = x.reshape(b * s, d)
    for l in range(depth):
        qkvu, gates = _in_proj(x2, norm_mix[l][None, :], w_in, b_gate[l][None, :], l)
        qkvu3 = qkvu.reshape(b, s, QKVU_COLS)
        gq = jnp.tile(q_norm_a[l], 2)[None, :]
        gk = jnp.tile(k_norm_a[l], 2)[None, :]
        o_a = _mixer_a(qkvu3, gq, gk, _rel_bias_units(rel_bias_a[l]))
        o_b = _mixer_b(qkvu3)
        x2 = _merge(x2, o_a.reshape(b * s, WIDTH), o_b.reshape(b * s, WIDTH), qkvu, gates,
                    w_pool, pool_scale[l][None, :], w_branch_a, w_branch_b, w_branch_c, w_out, s, l)
        x2 = _ffn(x2, norm_ffn[l][None, :], w_up, conv_w[l], conv_b[l][None, :], w_down, s, l)
    return x2.reshape(b, s, d)
```

```python
import functools
import math

import jax
import jax.numpy as jnp
import numpy as np
from jax import lax
from jax.experimental import pallas as pl
from jax.experimental.pallas import tpu as pltpu

D_MODEL = 1024
CHUNK = 64
N_LEFT_CHUNKS = 8
HEAD_DIM = 64
N_HEADS = 8
N_PAIRS = N_HEADS // 2
WIDTH = N_HEADS * HEAD_DIM
POOL_WINDOWS = (2, 4, 8, 16)
POOL_GROUP_DIM = 128
MAX_REL_DIST = 2 * CHUNK
D_FF = 2816
EPS = 1e-6
MASK_VALUE = -1e30

QKVU_COLS = 7 * WIDTH
GATE_COLS = 3 * D_MODEL

LANES = 128
VMEM_LIMIT_BYTES = 56 * 1024 * 1024

TM_PROJ = 512
TN_PROJ = 512
TM_FFN = 512
LOOKBACK_A = N_LEFT_CHUNKS * CHUNK
TQ_A = 2048
UNIT_A = 2 * CHUNK
BAND_A = (N_LEFT_CHUNKS + 2) * CHUNK
TQ_B = 256
TILES_B = 4
KB_B = 256
FF_CHUNK = 256
HALO_POOL = 16
HALO_CONV = 8

LOG2E = 1.4426950408889634
CARRY_FLOOR_B = -152.0

BF16 = jnp.bfloat16
F32 = jnp.float32


def _dot(a, b):
    return jnp.dot(a, b, preferred_element_type=F32)


def _dot_nt(a, b):
    return lax.dot_general(a, b, (((1,), (1,)), ((), ())), preferred_element_type=F32)


def _rms_norm_rows(x, gain):
    ms = jnp.mean(x * x, axis=-1, keepdims=True)
    return x * lax.rsqrt(ms + EPS) * gain


def _compiler_params(semantics):
    return pltpu.CompilerParams(dimension_semantics=semantics,
                                vmem_limit_bytes=VMEM_LIMIT_BYTES)


def _resident(shape):
    zeros = (0,) * len(shape)
    return pl.BlockSpec(shape, lambda *_: zeros, pipeline_mode=pl.Buffered(1))


def _layer_resident(layer, shape):
    index = (layer,) + (0,) * len(shape)
    return pl.BlockSpec((None,) + tuple(shape), lambda *_: index, pipeline_mode=pl.Buffered(1))


def _low_lanes():
    return lax.broadcasted_iota(jnp.int32, (1, LANES), 1) < HEAD_DIM


def _in_proj_kernel(x_ref, g_ref, w_ref, b_ref, qkvu_ref, gate_ref):
    h = _rms_norm_rows(x_ref[...], g_ref[...]).astype(BF16)
    for n0 in range(0, QKVU_COLS, TN_PROJ):
        qkvu_ref[:, n0:n0 + TN_PROJ] = _dot(h, w_ref[:, n0:n0 + TN_PROJ]).astype(BF16)
    for n0 in range(0, GATE_COLS, TN_PROJ):
        z = _dot(h, w_ref[:, QKVU_COLS + n0:QKVU_COLS + n0 + TN_PROJ]) + b_ref[:, n0:n0 + TN_PROJ]
        gate_ref[:, n0:n0 + TN_PROJ] = jax.nn.sigmoid(z).astype(BF16)


def _in_proj(x2, gain, w_in, b_gate, layer):
    t = x2.shape[0]
    return pl.pallas_call(
        _in_proj_kernel,
        grid=(t // TM_PROJ,),
        in_specs=[
            pl.BlockSpec((TM_PROJ, D_MODEL), lambda i: (i, 0)),
            _resident((1, D_MODEL)),
            _layer_resident(layer, (D_MODEL, QKVU_COLS + GATE_COLS)),
            _resident((1, GATE_COLS)),
        ],
        out_specs=[
            pl.BlockSpec((TM_PROJ, QKVU_COLS), lambda i: (i, 0)),
            pl.BlockSpec((TM_PROJ, GATE_COLS), lambda i: (i, 0)),
        ],
        out_shape=[
            jax.ShapeDtypeStruct((t, QKVU_COLS), BF16),
            jax.ShapeDtypeStruct((t, GATE_COLS), BF16),
        ],
        compiler_params=_compiler_params(("arbitrary",)),
        name="in_proj",
    )(x2, gain, w_in, b_gate)


def _head_pair_norm(x, gain, low_lanes):
    sq = x * x
    s_low = jnp.sum(jnp.where(low_lanes, sq, 0.0), axis=-1, keepdims=True)
    s_high = jnp.sum(jnp.where(low_lanes, 0.0, sq), axis=-1, keepdims=True)
    ms = jnp.where(low_lanes, s_low, s_high) * (1.0 / HEAD_DIM)
    return x * lax.rsqrt(ms + EPS) * gain


def _mixer_a_kernel(q_ref, k_ref, v_ref, gq_ref, gk_ref, bias_ref, o_ref, q_scr, kt_scr, v_scr):
    t = pl.program_id(2)
    low_lanes = _low_lanes()
    n_units = TQ_A // UNIT_A

    q = _head_pair_norm(q_ref[0].astype(F32), gq_ref[...], low_lanes) * (1.0 / math.sqrt(HEAD_DIM))
    for u in range(n_units):
        qu = q[u * UNIT_A:(u + 1) * UNIT_A]
        q_scr[u, 0:UNIT_A, :] = jnp.where(low_lanes, qu, 0.0).astype(BF16)
        q_scr[u, UNIT_A:2 * UNIT_A, :] = jnp.where(low_lanes, 0.0, qu).astype(BF16)

    @pl.when(t == 0)
    def _():
        kt_scr[:, 0:LOOKBACK_A] = jnp.zeros((LANES, LOOKBACK_A), BF16)
        v_scr[0:LOOKBACK_A, :] = jnp.zeros((LOOKBACK_A, LANES), BF16)

    @pl.when(t > 0)
    def _():
        kt_scr[:, 0:LOOKBACK_A] = kt_scr[:, TQ_A:TQ_A + LOOKBACK_A]
        v_scr[0:LOOKBACK_A, :] = v_scr[TQ_A:TQ_A + LOOKBACK_A, :]

    kn = _head_pair_norm(k_ref[0].astype(F32), gk_ref[...], low_lanes)
    kt_scr[:, LOOKBACK_A:LOOKBACK_A + TQ_A] = kn.T.astype(BF16)
    v_scr[LOOKBACK_A:LOOKBACK_A + TQ_A, :] = v_ref[0]

    key_idx = lax.broadcasted_iota(jnp.int32, (1, BAND_A), 1)

    def scores(u):
        return _dot(q_scr[u], kt_scr[:, u * UNIT_A:u * UNIT_A + BAND_A])

    s_next = scores(0)
    for u in range(n_units):
        k0 = u * UNIT_A
        s_now = s_next
        if u + 1 < n_units:
            s_next = scores(u + 1)
        before_start = jnp.logical_and(key_idx + k0 < LOOKBACK_A, t == 0)
        start_mask = jnp.where(before_start, MASK_VALUE, 0.0)
        logits = s_now + (bias_ref[0] + start_mask)
        m = jnp.max(logits, axis=-1, keepdims=True)
        p = jnp.exp(logits - m)
        denom = jnp.sum(p, axis=-1, keepdims=True)
        pv = _dot(p.astype(BF16), v_scr[k0:k0 + BAND_A, :]) / denom
        o_ref[0, k0:k0 + UNIT_A, :] = jnp.where(low_lanes, pv[0:UNIT_A], pv[UNIT_A:]).astype(BF16)


def _mixer_a(qkvu3, gq, gk, bias):
    b, s, _ = qkvu3.shape
    blk = (1, TQ_A, LANES)
    return pl.pallas_call(
        _mixer_a_kernel,
        grid=(b, N_PAIRS, s // TQ_A),
        in_specs=[
            pl.BlockSpec(blk, lambda bi, p, t: (bi, t, p)),
            pl.BlockSpec(blk, lambda bi, p, t: (bi, t, N_PAIRS + p)),
            pl.BlockSpec(blk, lambda bi, p, t: (bi, t, 2 * N_PAIRS + p)),
            _resident((1, LANES)),
            _resident((1, LANES)),
            pl.BlockSpec((1, 2 * UNIT_A, BAND_A), lambda bi, p, t: (p, 0, 0)),
        ],
        out_specs=pl.BlockSpec(blk, lambda bi, p, t: (bi, t, p)),
        out_shape=jax.ShapeDtypeStruct((b, s, WIDTH), BF16),
        scratch_shapes=[
            pltpu.VMEM((TQ_A // UNIT_A, 2 * UNIT_A, LANES), BF16),
            pltpu.VMEM((LANES, LOOKBACK_A + TQ_A), BF16),
            pltpu.VMEM((LOOKBACK_A + TQ_A, LANES), BF16),
        ],
        compiler_params=_compiler_params(("arbitrary", "arbitrary", "arbitrary")),
        name="mixer_a",
    )(qkvu3, qkvu3, qkvu3, gq, gk, bias)


def _rel_bias_units(rel_bias):
    period = UNIT_A + BAND_A
    d = np.arange(period)
    d = np.where(d < BAND_A, d, d - period)
    idx = np.clip(N_LEFT_CHUNKS * CHUNK - d, -(CHUNK - 1), MAX_REL_DIST) + (CHUNK - 1)
    vec = rel_bias.astype(F32)[:, idx]
    h = rel_bias.shape[0]
    rows = jnp.tile(vec, (1, UNIT_A))[:, :UNIT_A * (period - 1)]
    toeplitz = rows.reshape(h, UNIT_A, period - 1)[:, :, :BAND_A]
    q_chunk = np.arange(UNIT_A)[:, None] // CHUNK
    k_chunk = np.arange(BAND_A)[None, :] // CHUNK
    in_band = (k_chunk >= q_chunk) & (k_chunk <= q_chunk + N_LEFT_CHUNKS)
    table = jnp.where(in_band[None], toeplitz, MASK_VALUE)
    return table.reshape(h // 2, 2 * UNIT_A, BAND_A)


def _mixer_b_kernel(q_ref, k_ref, v_ref, cum_ref, o_ref, q_scr, acc_ref, carry_ref):
    m = pl.program_id(2)
    low_lanes = _low_lanes()
    for tile in range(TILES_B):
        q = q_ref[0, tile * TQ_B:(tile + 1) * TQ_B, :].astype(F32) * (1.0 / math.sqrt(HEAD_DIM))
        q_scr[tile, 0:TQ_B, :] = jnp.where(low_lanes, q, 0.0).astype(BF16)
        q_scr[tile, TQ_B:2 * TQ_B, :] = jnp.where(low_lanes, 0.0, q).astype(BF16)

    def visit(tile, j, carry, diagonal):
        start = pl.multiple_of(j * KB_B, KB_B)
        neg_z = _dot_nt(q_scr[tile], k_ref[0, pl.ds(start, KB_B), :]) * (-LOG2E)
        soft = jnp.log2(1.0 + jnp.exp2(-jnp.abs(neg_z)))
        log_keep = jnp.minimum(neg_z, 0.0) - soft
        log_beta = log_keep - neg_z
        if diagonal:
            row = lax.broadcasted_iota(jnp.int32, (2 * TQ_B, KB_B), 0)
            col = lax.broadcasted_iota(jnp.int32, (2 * TQ_B, KB_B), 1)
            before = col < jnp.where(row >= TQ_B, row - TQ_B, row)
            log_keep = jnp.where(before, log_keep, 0.0)
        hi = log_keep.astype(BF16)
        lo = (log_keep - hi.astype(F32)).astype(BF16)
        tail = _dot(jnp.concatenate([hi, lo], axis=1), cum_ref[...])
        exponent = log_beta + tail
        if carry is not None:
            exponent = exponent + jnp.concatenate([carry] * (KB_B // LANES), axis=1)
        w = jnp.exp2(exponent)
        if diagonal:
            w = jnp.where(before, w, 0.0)
        pv = _dot(w.astype(BF16), v_ref[0, pl.ds(start, KB_B), :])
        block_total = jnp.broadcast_to(tail[:, 0:1] + log_keep[:, 0:1], (2 * TQ_B, LANES))
        return pv, (block_total if carry is None else carry + block_total)

    def first_visits(tiles_with_previous):
        diag = {t: visit(t, TILES_B * m + t, None, True) for t in range(TILES_B)}
        for t in range(TILES_B):
            pv, carry = diag[t]
            if t in tiles_with_previous:
                pv_prev, carry = visit(t, TILES_B * m + t - 1, carry, False)
                pv = pv + pv_prev
            acc_ref[t] = pv
            carry_ref[t] = carry

    @pl.when(m == 0)
    def _():
        first_visits(range(1, TILES_B))

    @pl.when(m > 0)
    def _():
        first_visits(range(TILES_B))

    def is_live(carry):
        return (jnp.max(carry) >= CARRY_FLOOR_B).astype(jnp.int32)

    for t in range(TILES_B):
        diag_block = TILES_B * m + t

        def more_keys(state, diag_block=diag_block):
            step, live = state
            return jnp.logical_and(step < diag_block, live > 0)

        def body(state, t=t, diag_block=diag_block):
            step, _ = state
            pv, carry = visit(t, diag_block - 1 - step, carry_ref[t], False)
            acc_ref[t] += pv
            carry_ref[t] = carry
            return step + 1, is_live(carry)

        lax.while_loop(more_keys, body, (jnp.int32(1), is_live(carry_ref[t])))
        o_ref[0, t * TQ_B:(t + 1) * TQ_B, :] = jnp.where(
            low_lanes, acc_ref[t, 0:TQ_B, :], acc_ref[t, TQ_B:2 * TQ_B, :]).astype(BF16)


def _cumulative_matrix():
    j = np.arange(KB_B)[:, None]
    s = np.arange(KB_B)[None, :]
    strict = (j > s).astype(np.float32)
    return jnp.asarray(np.concatenate([strict, strict], axis=0), BF16)


def _mixer_b(qkvu3):
    b, s, _ = qkvu3.shape
    return pl.pallas_call(
        _mixer_b_kernel,
        grid=(b, N_PAIRS, s // (TILES_B * TQ_B)),
        in_specs=[
            pl.BlockSpec((1, TILES_B * TQ_B, LANES), lambda bi, p, i: (bi, i, 3 * N_PAIRS + p)),
            pl.BlockSpec((1, s, LANES), lambda bi, p, i: (bi, 0, 4 * N_PAIRS + p)),
            pl.BlockSpec((1, s, LANES), lambda bi, p, i: (bi, 0, 5 * N_PAIRS + p)),
            _resident((2 * KB_B, KB_B)),
        ],
        out_specs=pl.BlockSpec((1, TILES_B * TQ_B, LANES), lambda bi, p, i: (bi, i, p)),
        out_shape=jax.ShapeDtypeStruct((b, s, WIDTH), BF16),
        scratch_shapes=[
            pltpu.VMEM((TILES_B, 2 * TQ_B, LANES), BF16),
            pltpu.VMEM((TILES_B, 2 * TQ_B, LANES), F32),
            pltpu.VMEM((TILES_B, 2 * TQ_B, LANES), F32),
        ],
        compiler_params=_compiler_params(("arbitrary", "arbitrary", "arbitrary")),
        name="mixer_b",
    )(qkvu3, qkvu3, qkvu3, _cumulative_matrix())


def _merge_kernel(tiles_per_seq, x_ref, oa_ref, ob_ref, u_ref, uh_ref, gate_ref,
                  wp_ref, ps_ref, wa_ref, wb_ref, wc_ref, wo_ref, o_ref):
    first = (pl.program_id(0) % tiles_per_seq) == 0
    u = u_ref[...].astype(F32)
    halo = jnp.where(first, 0.0, uh_ref[...].astype(F32))
    ext = jnp.concatenate([halo, u], axis=0)
    pos = lax.broadcasted_iota(jnp.int32, (TM_PROJ, 1), 0)

    pooled_groups = []
    for g, win in enumerate(POOL_WINDOWS):
        cols = slice(g * POOL_GROUP_DIM, (g + 1) * POOL_GROUP_DIM)
        a = ext[:, cols]
        step = 1
        while step < win:
            a = a + pltpu.roll(a, step, 0)
            step *= 2
        win_sum = a[HALO_POOL:]
        count = jnp.where(first, jnp.minimum(pos + 1, win), win).astype(F32)
        pooled = win_sum / count - u[:, cols]
        mixed = _dot(pooled.astype(BF16), wp_ref[g])
        pooled_groups.append(mixed * ps_ref[:, cols])
    o_c = jnp.concatenate(pooled_groups, axis=1).astype(BF16)

    merged = (gate_ref[:, 0:D_MODEL].astype(F32) * _dot(oa_ref[...], wa_ref[...])
              + gate_ref[:, D_MODEL:2 * D_MODEL].astype(F32) * _dot(ob_ref[...], wb_ref[...])
              + gate_ref[:, 2 * D_MODEL:3 * D_MODEL].astype(F32) * _dot(o_c, wc_ref[...]))
    o_ref[...] = x_ref[...] + _dot(merged.astype(BF16), wo_ref[...])


def _merge(x2, o_a, o_b, qkvu, gates, w_pool, pool_scale, w_a, w_b, w_c, w_out, seq, layer):
    t = x2.shape[0]
    u_block = QKVU_COLS // WIDTH - 1
    halo_blocks = TM_PROJ // HALO_POOL
    return pl.pallas_call(
        functools.partial(_merge_kernel, seq // TM_PROJ),
        grid=(t // TM_PROJ,),
        in_specs=[
            pl.BlockSpec((TM_PROJ, D_MODEL), lambda i: (i, 0)),
            pl.BlockSpec((TM_PROJ, WIDTH), lambda i: (i, 0)),
            pl.BlockSpec((TM_PROJ, WIDTH), lambda i: (i, 0)),
            pl.BlockSpec((TM_PROJ, WIDTH), lambda i: (i, u_block)),
            pl.BlockSpec((HALO_POOL, WIDTH), lambda i: (jnp.maximum(i * halo_blocks - 1, 0), u_block)),
            pl.BlockSpec((TM_PROJ, GATE_COLS), lambda i: (i, 0)),
            _layer_resident(layer, (len(POOL_WINDOWS), POOL_GROUP_DIM, POOL_GROUP_DIM)),
            _resident((1, WIDTH)),
            _layer_resident(layer, (WIDTH, D_MODEL)),
            _layer_resident(layer, (WIDTH, D_MODEL)),
            _layer_resident(layer, (WIDTH, D_MODEL)),
            _layer_resident(layer, (D_MODEL, D_MODEL)),
        ],
        out_specs=pl.BlockSpec((TM_PROJ, D_MODEL), lambda i: (i, 0)),
        out_shape=jax.ShapeDtypeStruct((t, D_MODEL), F32),
        compiler_params=_compiler_params(("arbitrary",)),
        name="merge",
    )(x2, o_a, o_b, qkvu, qkvu, gates, w_pool, pool_scale, w_a, w_b, w_c, w_out)


def _ffn_kernel(tiles_per_seq, x_ref, g_ref, wu_ref, cw_ref, cb_ref, wd_ref, o_ref, halo_ref, h_ref):
    first = (pl.program_id(0) % tiles_per_seq) == 0
    x = x_ref[...]
    h_ref[...] = _rms_norm_rows(x, g_ref[...]).astype(BF16)
    o_ref[...] = x
    row = lax.broadcasted_iota(jnp.int32, (HALO_CONV, 1), 0)

    def up(n0):
        return tuple(_dot(h_ref[...], wu_ref[:, c0:c0 + FF_CHUNK]) for c0 in (n0, D_FF + n0))

    def conv(u, c0):
        cols = slice(c0, c0 + FF_CHUNK)
        prev = jnp.where(first, 0.0, halo_ref[:, cols])
        halo_ref[:, cols] = u[TM_FFN - HALO_CONV:]
        back1 = prev[HALO_CONV - 1:HALO_CONV]
        back2 = prev[HALO_CONV - 2:HALO_CONV - 1]
        r1 = pltpu.roll(u, 1, 0)
        r2 = pltpu.roll(u, 2, 0)
        top1 = jnp.where(row == 0, back1, r1[0:HALO_CONV])
        top2 = jnp.where(row == 0, back2, jnp.where(row == 1, back1, r2[0:HALO_CONV]))
        u1 = jnp.concatenate([top1, r1[HALO_CONV:]], axis=0)
        u2 = jnp.concatenate([top2, r2[HALO_CONV:]], axis=0)
        return (cb_ref[:, cols] + cw_ref[0:1, cols] * u2 + cw_ref[1:2, cols] * u1
                + cw_ref[2:3, cols] * u)

    chunks = list(range(0, D_FF, FF_CHUNK))
    u_next = up(chunks[0])
    for k, n0 in enumerate(chunks):
        u_gate, u_val = u_next
        if k + 1 < len(chunks):
            u_next = up(chunks[k + 1])
        act = (jax.nn.silu(conv(u_gate, n0)) * conv(u_val, D_FF + n0)).astype(BF16)
        o_ref[...] += _dot(act, wd_ref[n0:n0 + FF_CHUNK, :])


def _ffn(x2, gain, w_up, conv_w, conv_b, w_down, seq, layer):
    t = x2.shape[0]
    return pl.pallas_call(
        functools.partial(_ffn_kernel, seq // TM_FFN),
        grid=(t // TM_FFN,),
        in_specs=[
            pl.BlockSpec((TM_FFN, D_MODEL), lambda i: (i, 0)),
            _resident((1, D_MODEL)),
            _layer_resident(layer, (D_MODEL, 2 * D_FF)),
            _resident((3, 2 * D_FF)),
            _resident((1, 2 * D_FF)),
            _layer_resident(layer, (D_FF, D_MODEL)),
        ],
        out_specs=pl.BlockSpec((TM_FFN, D_MODEL), lambda i: (i, 0)),
        out_shape=jax.ShapeDtypeStruct((t, D_MODEL), F32),
        scratch_shapes=[pltpu.VMEM((HALO_CONV, 2 * D_FF), F32), pltpu.VMEM((TM_FFN, D_MODEL), BF16)],
        compiler_params=_compiler_params(("arbitrary",)),
        name="ffn",
    )(x2, gain, w_up, conv_w, conv_b, w_down)


def kernel(x, norm_mix, w_in, b_gate, q_norm_a, k_norm_a, rel_bias_a, w_pool, pool_scale,
           w_branch_a, w_branch_b, w_branch_c, w_out, norm_ffn, w_up, conv_w, conv_b, w_down):
    b, s, d = x.shape
    assert d == D_MODEL and s % TQ_A == 0 and s % TM_PROJ == 0 and s % (TILES_B * TQ_B) == 0
    depth = w_in.shape[0]
    w_in, w_pool, w_branch_a, w_branch_b, w_branch_c, w_out, w_up, w_down = (
        w.astype(BF16) for w in (w_in, w_pool, w_branch_a, w_branch_b, w_branch_c, w_out, w_up, w_down))
    x2 = x.reshape(b * s, d)
    for l in range(depth):
        qkvu, gates = _in_proj(x2, norm_mix[l][None, :], w_in, b_gate[l][None, :], l)
        qkvu3 = qkvu.reshape(b, s, QKVU_COLS)
        gq = jnp.tile(q_norm_a[l], 2)[None, :]
        gk = jnp.tile(k_norm_a[l], 2)[None, :]
        o_a = _mixer_a(qkvu3, gq, gk, _rel_bias_units(rel_bias_a[l]))
        o_b = _mixer_b(qkvu3)
        x2 = _merge(x2, o_a.reshape(b * s, WIDTH), o_b.reshape(b * s, WIDTH), qkvu, gates,
                    w_pool, pool_scale[l][None, :], w_branch_a, w_branch_b, w_branch_c, w_out, s, l)
        x2 = _ffn(x2, norm_ffn[l][None, :], w_up, conv_w[l], conv_b[l][None, :], w_down, s, l)
    return x2.reshape(b, s, d)
```

```python
import functools
import math

import jax
import jax.numpy as jnp
import numpy as np
from jax import lax
from jax.experimental import pallas as pl
from jax.experimental.pallas import tpu as pltpu

D_MODEL = 1024
CHUNK = 64
N_LEFT_CHUNKS = 8
HEAD_DIM = 64
N_HEADS = 8
N_PAIRS = N_HEADS // 2
WIDTH = N_HEADS * HEAD_DIM
POOL_WINDOWS = (2, 4, 8, 16)
POOL_GROUP_DIM = 128
MAX_REL_DIST = 2 * CHUNK
D_FF = 2816
EPS = 1e-6
MASK_VALUE = -1e30

QKVU_COLS = 7 * WIDTH
GATE_COLS = 3 * D_MODEL

LANES = 128
SUBLANES = 8
VMEM_LIMIT_BYTES = 56 * 1024 * 1024

TM_PROJ = 512
TN_PROJ = 512
TM_FFN = 512
LOOKBACK_A = N_LEFT_CHUNKS * CHUNK
TQ_A = 2048
UNIT_A = 2 * CHUNK
BAND_A = (N_LEFT_CHUNKS + 2) * CHUNK
GROUP_A = 4
TQ_B = 256
TILES_B = 4
GROUP_B = 4
KB_B = 256
FF_CHUNK = 256
HALO_POOL = 16

LOG2E = 1.4426950408889634
CARRY_FLOOR_B = -152.0

BF16 = jnp.bfloat16
F32 = jnp.float32


def _dot(a, b):
    return jnp.dot(a, b, preferred_element_type=F32)


def _dot_nt(a, b):
    return lax.dot_general(a, b, (((1,), (1,)), ((), ())), preferred_element_type=F32)


def _rms_norm_rows(x, gain):
    ms = jnp.mean(x * x, axis=-1, keepdims=True)
    return x * lax.rsqrt(ms + EPS) * gain


def _compiler_params(semantics):
    return pltpu.CompilerParams(dimension_semantics=semantics,
                                vmem_limit_bytes=VMEM_LIMIT_BYTES)


def _resident(shape):
    zeros = (0,) * len(shape)
    return pl.BlockSpec(shape, lambda *_: zeros, pipeline_mode=pl.Buffered(1))


def _layer_resident(layer, shape):
    index = (layer,) + (0,) * len(shape)
    return pl.BlockSpec((None,) + tuple(shape), lambda *_: index, pipeline_mode=pl.Buffered(1))


def _low_lanes():
    return lax.broadcasted_iota(jnp.int32, (1, LANES), 1) < HEAD_DIM


def _in_proj_kernel(x_ref, g_ref, w_ref, b_ref, qkvu_ref, gate_ref):
    h = _rms_norm_rows(x_ref[...], g_ref[...]).astype(BF16)
    for n0 in range(0, QKVU_COLS, TN_PROJ):
        qkvu_ref[:, n0:n0 + TN_PROJ] = _dot(h, w_ref[:, n0:n0 + TN_PROJ]).astype(BF16)
    for n0 in range(0, GATE_COLS, TN_PROJ):
        z = _dot(h, w_ref[:, QKVU_COLS + n0:QKVU_COLS + n0 + TN_PROJ]) + b_ref[:, n0:n0 + TN_PROJ]
        gate_ref[:, n0:n0 + TN_PROJ] = jax.nn.sigmoid(z).astype(BF16)


def _in_proj(x2, gain, w_in, b_gate, layer):
    t = x2.shape[0]
    return pl.pallas_call(
        _in_proj_kernel,
        grid=(t // TM_PROJ,),
        in_specs=[
            pl.BlockSpec((TM_PROJ, D_MODEL), lambda i: (i, 0)),
            _resident((1, D_MODEL)),
            _layer_resident(layer, (D_MODEL, QKVU_COLS + GATE_COLS)),
            _resident((1, GATE_COLS)),
        ],
        out_specs=[
            pl.BlockSpec((TM_PROJ, QKVU_COLS), lambda i: (i, 0)),
            pl.BlockSpec((TM_PROJ, GATE_COLS), lambda i: (i, 0)),
        ],
        out_shape=[
            jax.ShapeDtypeStruct((t, QKVU_COLS), BF16),
            jax.ShapeDtypeStruct((t, GATE_COLS), BF16),
        ],
        compiler_params=_compiler_params(("arbitrary",)),
        name="in_proj",
    )(x2, gain, w_in, b_gate)


def _head_pair_norm(x, gain, low_lanes):
    sq = x * x
    s_low = jnp.sum(jnp.where(low_lanes, sq, 0.0), axis=-1, keepdims=True)
    s_high = jnp.sum(jnp.where(low_lanes, 0.0, sq), axis=-1, keepdims=True)
    ms = jnp.where(low_lanes, s_low, s_high) * (1.0 / HEAD_DIM)
    return x * lax.rsqrt(ms + EPS) * gain


def _mixer_a_kernel(q_ref, k_ref, v_ref, gq_ref, gk_ref, bias_ref, o_ref, q_scr, kt_scr, v_scr):
    t = pl.program_id(2)
    low_lanes = _low_lanes()
    n_units = TQ_A // UNIT_A

    q = _head_pair_norm(q_ref[0].astype(F32), gq_ref[...], low_lanes) * (1.0 / math.sqrt(HEAD_DIM))
    for u in range(n_units):
        qu = q[u * UNIT_A:(u + 1) * UNIT_A]
        q_scr[u, 0:UNIT_A, :] = jnp.where(low_lanes, qu, 0.0).astype(BF16)
        q_scr[u, UNIT_A:2 * UNIT_A, :] = jnp.where(low_lanes, 0.0, qu).astype(BF16)

    @pl.when(t == 0)
    def _():
        kt_scr[:, 0:LOOKBACK_A] = jnp.zeros((LANES, LOOKBACK_A), BF16)
        v_scr[0:LOOKBACK_A, :] = jnp.zeros((LOOKBACK_A, LANES), BF16)

    @pl.when(t > 0)
    def _():
        kt_scr[:, 0:LOOKBACK_A] = kt_scr[:, TQ_A:TQ_A + LOOKBACK_A]
        v_scr[0:LOOKBACK_A, :] = v_scr[TQ_A:TQ_A + LOOKBACK_A, :]

    kn = _head_pair_norm(k_ref[0].astype(F32), gk_ref[...], low_lanes)
    kt_scr[:, LOOKBACK_A:LOOKBACK_A + TQ_A] = kn.T.astype(BF16)
    v_scr[LOOKBACK_A:LOOKBACK_A + TQ_A, :] = v_ref[0]

    key_idx = lax.broadcasted_iota(jnp.int32, (1, BAND_A), 1)

    for u0 in range(0, n_units, GROUP_A):
        group = range(u0, min(u0 + GROUP_A, n_units))
        probs, denoms = {}, {}
        for u in group:
            k0 = u * UNIT_A
            before_start = jnp.logical_and(key_idx + k0 < LOOKBACK_A, t == 0)
            start_mask = jnp.where(before_start, MASK_VALUE, 0.0)
            logits = _dot(q_scr[u], kt_scr[:, k0:k0 + BAND_A]) + (bias_ref[0] + start_mask)
            m = jnp.max(logits, axis=-1, keepdims=True)
            p = jnp.exp(logits - m)
            denoms[u] = jnp.sum(p, axis=-1, keepdims=True)
            probs[u] = p.astype(BF16)
        for u in group:
            k0 = u * UNIT_A
            pv = _dot(probs[u], v_scr[k0:k0 + BAND_A, :]) / denoms[u]
            o_ref[0, k0:k0 + UNIT_A, :] = jnp.where(low_lanes, pv[0:UNIT_A], pv[UNIT_A:]).astype(BF16)


def _mixer_a(qkvu3, gq, gk, bias):
    b, s, _ = qkvu3.shape
    blk = (1, TQ_A, LANES)
    return pl.pallas_call(
        _mixer_a_kernel,
        grid=(b, N_PAIRS, s // TQ_A),
        in_specs=[
            pl.BlockSpec(blk, lambda bi, p, t: (bi, t, p)),
            pl.BlockSpec(blk, lambda bi, p, t: (bi, t, N_PAIRS + p)),
            pl.BlockSpec(blk, lambda bi, p, t: (bi, t, 2 * N_PAIRS + p)),
            _resident((1, LANES)),
            _resident((1, LANES)),
            pl.BlockSpec((1, 2 * UNIT_A, BAND_A), lambda bi, p, t: (p, 0, 0)),
        ],
        out_specs=pl.BlockSpec(blk, lambda bi, p, t: (bi, t, p)),
        out_shape=jax.ShapeDtypeStruct((b, s, WIDTH), BF16),
        scratch_shapes=[
            pltpu.VMEM((TQ_A // UNIT_A, 2 * UNIT_A, LANES), BF16),
            pltpu.VMEM((LANES, LOOKBACK_A + TQ_A), BF16),
            pltpu.VMEM((LOOKBACK_A + TQ_A, LANES), BF16),
        ],
        compiler_params=_compiler_params(("arbitrary", "arbitrary", "arbitrary")),
        name="mixer_a",
    )(qkvu3, qkvu3, qkvu3, gq, gk, bias)


def _rel_bias_units(rel_bias):
    period = UNIT_A + BAND_A
    d = np.arange(period)
    d = np.where(d < BAND_A, d, d - period)
    idx = np.clip(N_LEFT_CHUNKS * CHUNK - d, -(CHUNK - 1), MAX_REL_DIST) + (CHUNK - 1)
    vec = rel_bias.astype(F32)[:, idx]
    h = rel_bias.shape[0]
    rows = jnp.tile(vec, (1, UNIT_A))[:, :UNIT_A * (period - 1)]
    toeplitz = rows.reshape(h, UNIT_A, period - 1)[:, :, :BAND_A]
    q_chunk = np.arange(UNIT_A)[:, None] // CHUNK
    k_chunk = np.arange(BAND_A)[None, :] // CHUNK
    in_band = (k_chunk >= q_chunk) & (k_chunk <= q_chunk + N_LEFT_CHUNKS)
    table = jnp.where(in_band[None], toeplitz, MASK_VALUE)
    return table.reshape(h // 2, 2 * UNIT_A, BAND_A)


def _mixer_b_kernel(q_ref, k_ref, v_ref, cum_ref, o_ref, q_scr, acc_ref, carry_ref):
    m = pl.program_id(2)
    low_lanes = _low_lanes()
    for tile in range(TILES_B):
        q = q_ref[0, tile * TQ_B:(tile + 1) * TQ_B, :].astype(F32) * (1.0 / math.sqrt(HEAD_DIM))
        q_scr[tile, 0:TQ_B, :] = jnp.where(low_lanes, q, 0.0).astype(BF16)
        q_scr[tile, TQ_B:2 * TQ_B, :] = jnp.where(low_lanes, 0.0, q).astype(BF16)

    def block_start(j):
        return pl.multiple_of(j * KB_B, KB_B)

    def scores(tile, j, diagonal):
        neg_z = _dot_nt(q_scr[tile], k_ref[0, pl.ds(block_start(j), KB_B), :]) * (-LOG2E)
        soft = jnp.log2(1.0 + jnp.exp2(-jnp.abs(neg_z)))
        log_keep = jnp.minimum(neg_z, 0.0) - soft
        log_beta = log_keep - neg_z
        before = None
        if diagonal:
            row = lax.broadcasted_iota(jnp.int32, (2 * TQ_B, KB_B), 0)
            col = lax.broadcasted_iota(jnp.int32, (2 * TQ_B, KB_B), 1)
            before = col < jnp.where(row >= TQ_B, row - TQ_B, row)
            log_keep = jnp.where(before, log_keep, 0.0)
        hi = log_keep.astype(BF16)
        lo = (log_keep - hi.astype(F32)).astype(BF16)
        return jnp.concatenate([hi, lo], axis=1), log_beta, log_keep[:, 0:1], before

    def weights(scored, carry):
        hi_lo, log_beta, first_keep, before = scored
        tail = _dot(hi_lo, cum_ref[...])
        exponent = log_beta + tail
        if carry is not None:
            exponent = exponent + jnp.concatenate([carry] * (KB_B // LANES), axis=1)
        w = jnp.exp2(exponent)
        if before is not None:
            w = jnp.where(before, w, 0.0)
        block_total = jnp.broadcast_to(tail[:, 0:1] + first_keep, (2 * TQ_B, LANES))
        return w.astype(BF16), (block_total if carry is None else carry + block_total)

    def weighted_values(w, j):
        return _dot(w, v_ref[0, pl.ds(block_start(j), KB_B), :])

    def first_visits(tiles, tiles_with_previous):
        diag_block = {t: TILES_B * m + t for t in tiles}
        scored_diag = {t: scores(t, diag_block[t], True) for t in tiles}
        scored_prev = {t: scores(t, diag_block[t] - 1, False) for t in tiles_with_previous}
        w_diag, w_prev, carry = {}, {}, {}
        for t in tiles:
            w_diag[t], carry[t] = weights(scored_diag[t], None)
        for t in tiles_with_previous:
            w_prev[t], carry[t] = weights(scored_prev[t], carry[t])
        for t in tiles:
            pv = weighted_values(w_diag[t], diag_block[t])
            if t in tiles_with_previous:
                pv = pv + weighted_values(w_prev[t], diag_block[t] - 1)
            acc_ref[t] = pv
            carry_ref[t] = carry[t]

    def first_visits_grouped(first_tile_has_previous):
        for t0 in range(0, TILES_B, GROUP_B):
            tiles = list(range(t0, min(t0 + GROUP_B, TILES_B)))
            first_visits(tiles, [t for t in tiles if t > 0 or first_tile_has_previous])

    @pl.when(m == 0)
    def _():
        first_visits_grouped(False)

    @pl.when(m > 0)
    def _():
        first_visits_grouped(True)

    def is_live(carry):
        return (jnp.max(carry) >= CARRY_FLOOR_B).astype(jnp.int32)

    for t in range(TILES_B):
        diag_block = TILES_B * m + t

        def more_keys(state, diag_block=diag_block):
            step, live = state
            return jnp.logical_and(step < diag_block, live > 0)

        def body(state, t=t, diag_block=diag_block):
            step, _ = state
            j = diag_block - 1 - step
            w, carry = weights(scores(t, j, False), carry_ref[t])
            acc_ref[t] += weighted_values(w, j)
            carry_ref[t] = carry
            return step + 1, is_live(carry)

        lax.while_loop(more_keys, body, (jnp.int32(1), is_live(carry_ref[t])))
        o_ref[0, t * TQ_B:(t + 1) * TQ_B, :] = jnp.where(
            low_lanes, acc_ref[t, 0:TQ_B, :], acc_ref[t, TQ_B:2 * TQ_B, :]).astype(BF16)


def _cumulative_matrix():
    j = np.arange(KB_B)[:, None]
    s = np.arange(KB_B)[None, :]
    strict = (j > s).astype(np.float32)
    return jnp.asarray(np.concatenate([strict, strict], axis=0), BF16)


def _mixer_b(qkvu3):
    b, s, _ = qkvu3.shape
    return pl.pallas_call(
        _mixer_b_kernel,
        grid=(b, N_PAIRS, s // (TILES_B * TQ_B)),
        in_specs=[
            pl.BlockSpec((1, TILES_B * TQ_B, LANES), lambda bi, p, i: (bi, i, 3 * N_PAIRS + p)),
            pl.BlockSpec((1, s, LANES), lambda bi, p, i: (bi, 0, 4 * N_PAIRS + p)),
            pl.BlockSpec((1, s, LANES), lambda bi, p, i: (bi, 0, 5 * N_PAIRS + p)),
            _resident((2 * KB_B, KB_B)),
        ],
        out_specs=pl.BlockSpec((1, TILES_B * TQ_B, LANES), lambda bi, p, i: (bi, i, p)),
        out_shape=jax.ShapeDtypeStruct((b, s, WIDTH), BF16),
        scratch_shapes=[
            pltpu.VMEM((TILES_B, 2 * TQ_B, LANES), BF16),
            pltpu.VMEM((TILES_B, 2 * TQ_B, LANES), F32),
            pltpu.VMEM((TILES_B, 2 * TQ_B, LANES), F32),
        ],
        compiler_params=_compiler_params(("arbitrary", "arbitrary", "arbitrary")),
        name="mixer_b",
    )(qkvu3, qkvu3, qkvu3, _cumulative_matrix())


def _merge_kernel(tiles_per_seq, x_ref, oa_ref, ob_ref, u_ref, uh_ref, gate_ref,
                  wp_ref, ps_ref, wa_ref, wb_ref, wc_ref, wo_ref, o_ref):
    first = (pl.program_id(0) % tiles_per_seq) == 0
    u = u_ref[...].astype(F32)
    halo = jnp.where(first, 0.0, uh_ref[...].astype(F32))
    ext = jnp.concatenate([halo, u], axis=0)
    pos = lax.broadcasted_iota(jnp.int32, (TM_PROJ, 1), 0)

    pooled_groups = []
    for g, win in enumerate(POOL_WINDOWS):
        cols = slice(g * POOL_GROUP_DIM, (g + 1) * POOL_GROUP_DIM)
        a = ext[:, cols]
        step = 1
        while step < win:
            a = a + pltpu.roll(a, step, 0)
            step *= 2
        win_sum = a[HALO_POOL:]
        count = jnp.where(first, jnp.minimum(pos + 1, win), win).astype(F32)
        pooled = win_sum / count - u[:, cols]
        mixed = _dot(pooled.astype(BF16), wp_ref[g])
        pooled_groups.append(mixed * ps_ref[:, cols])
    o_c = jnp.concatenate(pooled_groups, axis=1).astype(BF16)

    merged = (gate_ref[:, 0:D_MODEL].astype(F32) * _dot(oa_ref[...], wa_ref[...])
              + gate_ref[:, D_MODEL:2 * D_MODEL].astype(F32) * _dot(ob_ref[...], wb_ref[...])
              + gate_ref[:, 2 * D_MODEL:3 * D_MODEL].astype(F32) * _dot(o_c, wc_ref[...]))
    o_ref[...] = x_ref[...] + _dot(merged.astype(BF16), wo_ref[...])


def _merge(x2, o_a, o_b, qkvu, gates, w_pool, pool_scale, w_a, w_b, w_c, w_out, seq, layer):
    t = x2.shape[0]
    u_block = QKVU_COLS // WIDTH - 1
    halo_blocks = TM_PROJ // HALO_POOL
    return pl.pallas_call(
        functools.partial(_merge_kernel, seq // TM_PROJ),
        grid=(t // TM_PROJ,),
        in_specs=[
            pl.BlockSpec((TM_PROJ, D_MODEL), lambda i: (i, 0)),
            pl.BlockSpec((TM_PROJ, WIDTH), lambda i: (i, 0)),
            pl.BlockSpec((TM_PROJ, WIDTH), lambda i: (i, 0)),
            pl.BlockSpec((TM_PROJ, WIDTH), lambda i: (i, u_block)),
            pl.BlockSpec((HALO_POOL, WIDTH), lambda i: (jnp.maximum(i * halo_blocks - 1, 0), u_block)),
            pl.BlockSpec((TM_PROJ, GATE_COLS), lambda i: (i, 0)),
            _layer_resident(layer, (len(POOL_WINDOWS), POOL_GROUP_DIM, POOL_GROUP_DIM)),
            _resident((1, WIDTH)),
            _layer_resident(layer, (WIDTH, D_MODEL)),
            _layer_resident(layer, (WIDTH, D_MODEL)),
            _layer_resident(layer, (WIDTH, D_MODEL)),
            _layer_resident(layer, (D_MODEL, D_MODEL)),
        ],
        out_specs=pl.BlockSpec((TM_PROJ, D_MODEL), lambda i: (i, 0)),
        out_shape=jax.ShapeDtypeStruct((t, D_MODEL), F32),
        compiler_params=_compiler_params(("arbitrary",)),
        name="merge",
    )(x2, o_a, o_b, qkvu, qkvu, gates, w_pool, pool_scale, w_a, w_b, w_c, w_out)


def _ffn_kernel(tiles_per_seq, x_ref, g_ref, wu_ref, cw_ref, cb_ref, wd_ref, o_ref,
                halo_ref, h_ref, acc_ref, act_ref):
    first = (pl.program_id(0) % tiles_per_seq) == 0
    groups = TM_FFN // SUBLANES
    x = jnp.swapaxes(x_ref[...].reshape(SUBLANES, groups, D_MODEL), 0, 1).reshape(TM_FFN, D_MODEL)
    h_ref[...] = _rms_norm_rows(x, g_ref[...]).astype(BF16)
    acc_ref[...] = x
    row = lax.broadcasted_iota(jnp.int32, (SUBLANES, 1), 0)

    def up(n0):
        return tuple(_dot(h_ref[...], wu_ref[:, c0:c0 + FF_CHUNK]) for c0 in (n0, D_FF + n0))

    def conv(u, c0):
        cols = slice(c0, c0 + FF_CHUNK)
        prev = jnp.where(first, 0.0, halo_ref[:, cols])
        halo_ref[:, cols] = u[TM_FFN - 2 * SUBLANES:]
        back2 = prev[SUBLANES - 1:SUBLANES]
        back1 = prev[2 * SUBLANES - 1:2 * SUBLANES]
        top1 = jnp.where(row == 0, back1, pltpu.roll(u[TM_FFN - SUBLANES:], 1, 0))
        top2 = jnp.where(row == 0, back2, pltpu.roll(u[TM_FFN - 2 * SUBLANES:TM_FFN - SUBLANES], 1, 0))
        u1 = jnp.concatenate([top1, u[:TM_FFN - SUBLANES]], axis=0)
        u2 = jnp.concatenate([top2, top1, u[:TM_FFN - 2 * SUBLANES]], axis=0)
        return (cb_ref[:, cols] + cw_ref[0:1, cols] * u2 + cw_ref[1:2, cols] * u1
                + cw_ref[2:3, cols] * u)

    chunks = list(range(0, D_FF, FF_CHUNK))
    u_next = up(chunks[0])
    for k, n0 in enumerate(chunks):
        u_gate, u_val = u_next
        if k + 1 < len(chunks):
            u_next = up(chunks[k + 1])
        act = (jax.nn.silu(conv(u_gate, n0)) * conv(u_val, D_FF + n0)).astype(BF16)
        act_ref[:, n0:n0 + FF_CHUNK] = act
    out = acc_ref[...] + _dot(act_ref[...], wd_ref[...])
    o_ref[...] = jnp.swapaxes(out.reshape(groups, SUBLANES, D_MODEL), 0, 1).reshape(TM_FFN, D_MODEL)


def _ffn(x2, gain, w_up, conv_w, conv_b, w_down, seq, layer):
    t = x2.shape[0]
    return pl.pallas_call(
        functools.partial(_ffn_kernel, seq // TM_FFN),
        grid=(t // TM_FFN,),
        in_specs=[
            pl.BlockSpec((TM_FFN, D_MODEL), lambda i: (i, 0)),
            _resident((1, D_MODEL)),
            _layer_resident(layer, (D_MODEL, 2 * D_FF)),
            _resident((3, 2 * D_FF)),
            _resident((1, 2 * D_FF)),
            _layer_resident(layer, (D_FF, D_MODEL)),
        ],
        out_specs=pl.BlockSpec((TM_FFN, D_MODEL), lambda i: (i, 0)),
        out_shape=jax.ShapeDtypeStruct((t, D_MODEL), F32),
        scratch_shapes=[pltpu.VMEM((2 * SUBLANES, 2 * D_FF), F32), pltpu.VMEM((TM_FFN, D_MODEL), BF16),
                        pltpu.VMEM((TM_FFN, D_MODEL), F32), pltpu.VMEM((TM_FFN, D_FF), BF16)],
        compiler_params=_compiler_params(("arbitrary",)),
        name="ffn",
    )(x2, gain, w_up, conv_w, conv_b, w_down)


def kernel(x, norm_mix, w_in, b_gate, q_norm_a, k_norm_a, rel_bias_a, w_pool, pool_scale,
           w_branch_a, w_branch_b, w_branch_c, w_out, norm_ffn, w_up, conv_w, conv_b, w_down):
    b, s, d = x.shape
    assert d == D_MODEL and s % TQ_A == 0 and s % TM_PROJ == 0 and s % (TILES_B * TQ_B) == 0
    depth = w_in.shape[0]
    w_in, w_pool, w_branch_a, w_branch_b, w_branch_c, w_out, w_up, w_down = (
        w.astype(BF16) for w in (w_in, w_pool, w_branch_a, w_branch_b, w_branch_c, w_out, w_up, w_down))
    x2 = x.reshape(b * s, d)
    for l in range(depth):
        qkvu, gates = _in_proj(x2, norm_mix[l][None, :], w_in, b_gate[l][None, :], l)
        qkvu3 = qkvu.reshape(b, s, QKVU_COLS)
        gq = jnp.tile(q_norm_a[l], 2)[None, :]
        gk = jnp.tile(k_norm_a[l], 2)[None, :]
        o_a = _mixer_a(qkvu3, gq, gk, _rel_bias_units(rel_bias_a[l]))
        o_b = _mixer_b(qkvu3)
        x2 = _merge(x2, o_a.reshape(b * s, WIDTH), o_b.reshape(b * s, WIDTH), qkvu, gates,
                    w_pool, pool_scale[l][None, :], w_branch_a, w_branch_b, w_branch_c, w_out, s, l)
        x2 = _ffn(x2, norm_ffn[l][None, :], w_up, conv_w[l], conv_b[l][None, :], w_down, s, l)
    return x2.reshape(b, s, d)
```

```python
import functools
import math

import jax
import jax.numpy as jnp
import numpy as np
from jax import lax
from jax.experimental import pallas as pl
from jax.experimental.pallas import tpu as pltpu

D_MODEL = 1024
CHUNK = 64
N_LEFT_CHUNKS = 8
HEAD_DIM = 64
N_HEADS = 8
N_PAIRS = N_HEADS // 2
WIDTH = N_HEADS * HEAD_DIM
POOL_WINDOWS = (2, 4, 8, 16)
POOL_GROUP_DIM = 128
MAX_REL_DIST = 2 * CHUNK
D_FF = 2816
EPS = 1e-6
MASK_VALUE = -1e30

QKVU_COLS = 7 * WIDTH
GATE_COLS = 3 * D_MODEL

LANES = 128
SUBLANES = 8
VMEM_LIMIT_BYTES = 56 * 1024 * 1024

TM_PROJ = 1024
TN_PROJ = 512
TM_FFN = 1024
LOOKBACK_A = N_LEFT_CHUNKS * CHUNK
TQ_A = 2048
UNIT_A = 2 * CHUNK
BAND_A = (N_LEFT_CHUNKS + 2) * CHUNK
GROUP_A = 4
START_UNITS_A = LOOKBACK_A // UNIT_A
TQ_B = 256
TILES_B = 8
GROUP_B = 4
KB_B = 256
FF_CHUNK = 256
HALO_POOL = 16

LOG2E = 1.4426950408889634
CARRY_FLOOR_B = -152.0

BF16 = jnp.bfloat16
F32 = jnp.float32


def _dot(a, b):
    return jnp.dot(a, b, preferred_element_type=F32)


def _dot_nt(a, b):
    return lax.dot_general(a, b, (((1,), (1,)), ((), ())), preferred_element_type=F32)


def _rms_norm_rows(x, gain):
    ms = jnp.mean(x * x, axis=-1, keepdims=True)
    return x * lax.rsqrt(ms + EPS) * gain


def _compiler_params(semantics):
    return pltpu.CompilerParams(dimension_semantics=semantics,
                                vmem_limit_bytes=VMEM_LIMIT_BYTES)


def _resident(shape):
    zeros = (0,) * len(shape)
    return pl.BlockSpec(shape, lambda *_: zeros, pipeline_mode=pl.Buffered(1))


def _layer_resident(layer, shape):
    index = (layer,) + (0,) * len(shape)
    return pl.BlockSpec((None,) + tuple(shape), lambda *_: index, pipeline_mode=pl.Buffered(1))


def _head_pair_norm(x, gain, low_lanes):
    sq = x * x
    s_low = jnp.sum(jnp.where(low_lanes, sq, 0.0), axis=-1, keepdims=True)
    s_high = jnp.sum(jnp.where(low_lanes, 0.0, sq), axis=-1, keepdims=True)
    ms = jnp.where(low_lanes, s_low, s_high) * (1.0 / HEAD_DIM)
    return x * lax.rsqrt(ms + EPS) * gain


def _low_lanes():
    return lax.broadcasted_iota(jnp.int32, (1, LANES), 1) < HEAD_DIM


def _in_proj_kernel(x_ref, g_ref, w_ref, b_ref, gq_ref, gk_ref, qkvu_ref, gate_ref):
    h = _rms_norm_rows(x_ref[...], g_ref[...]).astype(BF16)
    low_lanes = _low_lanes()
    head_norm = {0: (gq_ref, 1.0 / math.sqrt(HEAD_DIM)), WIDTH: (gk_ref, 1.0)}
    for n0 in range(0, QKVU_COLS, TN_PROJ):
        y = _dot(h, w_ref[:, n0:n0 + TN_PROJ])
        if n0 in head_norm:
            gain_ref, scale = head_norm[n0]
            y = jnp.concatenate(
                [_head_pair_norm(y[:, c:c + LANES], gain_ref[...], low_lanes) * scale
                 for c in range(0, TN_PROJ, LANES)], axis=1)
        qkvu_ref[:, n0:n0 + TN_PROJ] = y.astype(BF16)
    for n0 in range(0, GATE_COLS, TN_PROJ):
        z = _dot(h, w_ref[:, QKVU_COLS + n0:QKVU_COLS + n0 + TN_PROJ]) + b_ref[:, n0:n0 + TN_PROJ]
        gate_ref[:, n0:n0 + TN_PROJ] = jax.nn.sigmoid(z).astype(BF16)


def _in_proj(x2, gain, w_in, b_gate, gq, gk, layer):
    t = x2.shape[0]
    return pl.pallas_call(
        _in_proj_kernel,
        grid=(t // TM_PROJ,),
        in_specs=[
            pl.BlockSpec((TM_PROJ, D_MODEL), lambda i: (i, 0)),
            _resident((1, D_MODEL)),
            _layer_resident(layer, (D_MODEL, QKVU_COLS + GATE_COLS)),
            _resident((1, GATE_COLS)),
            _resident((1, LANES)),
            _resident((1, LANES)),
        ],
        out_specs=[
            pl.BlockSpec((TM_PROJ, QKVU_COLS), lambda i: (i, 0)),
            pl.BlockSpec((TM_PROJ, GATE_COLS), lambda i: (i, 0)),
        ],
        out_shape=[
            jax.ShapeDtypeStruct((t, QKVU_COLS), BF16),
            jax.ShapeDtypeStruct((t, GATE_COLS), BF16),
        ],
        compiler_params=_compiler_params(("arbitrary",)),
        name="in_proj",
    )(x2, gain, w_in, b_gate, gq, gk)


def _mixer_a_kernel(q_ref, k_ref, v_ref, bias_ref, o_ref, q_scr, kt_scr, v_scr):
    t = pl.program_id(2)
    low_lanes = _low_lanes()
    n_units = TQ_A // UNIT_A

    zero = jnp.zeros((), BF16)
    for u in range(n_units):
        qu = q_ref[0, u * UNIT_A:(u + 1) * UNIT_A, :]
        q_scr[u, 0:UNIT_A, :] = jnp.where(low_lanes, qu, zero)
        q_scr[u, UNIT_A:2 * UNIT_A, :] = jnp.where(low_lanes, zero, qu)

    @pl.when(t == 0)
    def _():
        kt_scr[:, 0:LOOKBACK_A] = jnp.zeros((LANES, LOOKBACK_A), BF16)
        v_scr[0:LOOKBACK_A, :] = jnp.zeros((LOOKBACK_A, LANES), BF16)

    @pl.when(t > 0)
    def _():
        kt_scr[:, 0:LOOKBACK_A] = kt_scr[:, TQ_A:TQ_A + LOOKBACK_A]
        v_scr[0:LOOKBACK_A, :] = v_scr[TQ_A:TQ_A + LOOKBACK_A, :]

    kt_scr[:, LOOKBACK_A:LOOKBACK_A + TQ_A] = k_ref[0].astype(F32).T.astype(BF16)
    v_scr[LOOKBACK_A:LOOKBACK_A + TQ_A, :] = v_ref[0]

    for u0 in range(0, n_units, GROUP_A):
        group = range(u0, min(u0 + GROUP_A, n_units))
        probs, denoms = {}, {}
        for u in group:
            k0 = u * UNIT_A
            variant = jnp.where(t == 0, u + 1, 0) if u < START_UNITS_A else 0
            logits = _dot(q_scr[u], kt_scr[:, k0:k0 + BAND_A]) + bias_ref[0, variant]
            m = jnp.max(logits, axis=-1, keepdims=True)
            p = jnp.exp(logits - m)
            denoms[u] = jnp.sum(p, axis=-1, keepdims=True)
            probs[u] = p.astype(BF16)
        for u in group:
            k0 = u * UNIT_A
            pv = _dot(probs[u], v_scr[k0:k0 + BAND_A, :]) / denoms[u]
            o_ref[0, k0:k0 + UNIT_A, :] = jnp.where(low_lanes, pv[0:UNIT_A], pv[UNIT_A:]).astype(BF16)


def _mixer_a(qkvu3, bias):
    b, s, _ = qkvu3.shape
    blk = (1, TQ_A, LANES)
    return pl.pallas_call(
        _mixer_a_kernel,
        grid=(b, N_PAIRS, s // TQ_A),
        in_specs=[
            pl.BlockSpec(blk, lambda bi, p, t: (bi, t, p)),
            pl.BlockSpec(blk, lambda bi, p, t: (bi, t, N_PAIRS + p)),
            pl.BlockSpec(blk, lambda bi, p, t: (bi, t, 2 * N_PAIRS + p)),
            pl.BlockSpec((1, 1 + START_UNITS_A, 2 * UNIT_A, BAND_A), lambda bi, p, t: (p, 0, 0, 0)),
        ],
        out_specs=pl.BlockSpec(blk, lambda bi, p, t: (bi, t, p)),
        out_shape=jax.ShapeDtypeStruct((b, s, WIDTH), BF16),
        scratch_shapes=[
            pltpu.VMEM((TQ_A // UNIT_A, 2 * UNIT_A, LANES), BF16),
            pltpu.VMEM((LANES, LOOKBACK_A + TQ_A), BF16),
            pltpu.VMEM((LOOKBACK_A + TQ_A, LANES), BF16),
        ],
        compiler_params=_compiler_params(("arbitrary", "arbitrary", "arbitrary")),
        name="mixer_a",
    )(qkvu3, qkvu3, qkvu3, bias)


def _rel_bias_units(rel_bias):
    period = UNIT_A + BAND_A
    d = np.arange(period)
    d = np.where(d < BAND_A, d, d - period)
    idx = np.clip(N_LEFT_CHUNKS * CHUNK - d, -(CHUNK - 1), MAX_REL_DIST) + (CHUNK - 1)
    vec = rel_bias.astype(F32)[:, idx]
    h = rel_bias.shape[0]
    rows = jnp.tile(vec, (1, UNIT_A))[:, :UNIT_A * (period - 1)]
    toeplitz = rows.reshape(h, UNIT_A, period - 1)[:, :, :BAND_A]
    q_chunk = np.arange(UNIT_A)[:, None] // CHUNK
    k_chunk = np.arange(BAND_A)[None, :] // CHUNK
    in_band = (k_chunk >= q_chunk) & (k_chunk <= q_chunk + N_LEFT_CHUNKS)
    table = jnp.where(in_band[None], toeplitz, MASK_VALUE)
    variants = [table]
    for u in range(START_UNITS_A):
        before_start = (np.arange(BAND_A) + u * UNIT_A < LOOKBACK_A)[None, None, :]
        variants.append(jnp.where(before_start, MASK_VALUE, table))
    stacked = jnp.stack(variants, axis=1)
    pairs = stacked.reshape(h // 2, 2, 1 + START_UNITS_A, UNIT_A, BAND_A)
    return jnp.swapaxes(pairs, 1, 2).reshape(h // 2, 1 + START_UNITS_A, 2 * UNIT_A, BAND_A)


def _mixer_b_kernel(q_ref, k_ref, v_ref, cum_ref, o_ref, q_scr, acc_ref, carry_ref):
    m = pl.program_id(2)
    low_lanes = _low_lanes()
    for tile in range(TILES_B):
        q = q_ref[0, tile * TQ_B:(tile + 1) * TQ_B, :].astype(F32) * (1.0 / math.sqrt(HEAD_DIM))
        q_scr[tile, 0:TQ_B, :] = jnp.where(low_lanes, q, 0.0).astype(BF16)
        q_scr[tile, TQ_B:2 * TQ_B, :] = jnp.where(low_lanes, 0.0, q).astype(BF16)

    def block_start(j):
        return pl.multiple_of(j * KB_B, KB_B)

    def scores(tile, j, diagonal):
        neg_z = _dot_nt(q_scr[tile], k_ref[0, pl.ds(block_start(j), KB_B), :]) * (-LOG2E)
        soft = jnp.log2(1.0 + jnp.exp2(-jnp.abs(neg_z)))
        log_keep = jnp.minimum(neg_z, 0.0) - soft
        log_beta = log_keep - neg_z
        before = None
        if diagonal:
            row = lax.broadcasted_iota(jnp.int32, (2 * TQ_B, KB_B), 0)
            col = lax.broadcasted_iota(jnp.int32, (2 * TQ_B, KB_B), 1)
            before = col < jnp.where(row >= TQ_B, row - TQ_B, row)
            log_keep = jnp.where(before, log_keep, 0.0)
        hi = log_keep.astype(BF16)
        lo = (log_keep - hi.astype(F32)).astype(BF16)
        return jnp.concatenate([hi, lo], axis=1), log_beta, log_keep[:, 0:1], before

    def weights(scored, carry):
        hi_lo, log_beta, first_keep, before = scored
        tail = _dot(hi_lo, cum_ref[...])
        exponent = log_beta + tail
        if carry is not None:
            exponent = exponent + jnp.concatenate([carry] * (KB_B // LANES), axis=1)
        w = jnp.exp2(exponent)
        if before is not None:
            w = jnp.where(before, w, 0.0)
        block_total = jnp.broadcast_to(tail[:, 0:1] + first_keep, (2 * TQ_B, LANES))
        return w.astype(BF16), (block_total if carry is None else carry + block_total)

    def weighted_values(w, j):
        return _dot(w, v_ref[0, pl.ds(block_start(j), KB_B), :])

    def first_visits(tiles, tiles_with_previous):
        diag_block = {t: TILES_B * m + t for t in tiles}
        scored_diag = {t: scores(t, diag_block[t], True) for t in tiles}
        scored_prev = {t: scores(t, diag_block[t] - 1, False) for t in tiles_with_previous}
        w_diag, w_prev, carry = {}, {}, {}
        for t in tiles:
            w_diag[t], carry[t] = weights(scored_diag[t], None)
        for t in tiles_with_previous:
            w_prev[t], carry[t] = weights(scored_prev[t], carry[t])
        for t in tiles:
            pv = weighted_values(w_diag[t], diag_block[t])
            if t in tiles_with_previous:
                pv = pv + weighted_values(w_prev[t], diag_block[t] - 1)
            acc_ref[t] = pv
            carry_ref[t] = carry[t]

    def first_visits_grouped(first_tile_has_previous):
        for t0 in range(0, TILES_B, GROUP_B):
            tiles = list(range(t0, min(t0 + GROUP_B, TILES_B)))
            first_visits(tiles, [t for t in tiles if t > 0 or first_tile_has_previous])

    @pl.when(m == 0)
    def _():
        first_visits_grouped(False)

    @pl.when(m > 0)
    def _():
        first_visits_grouped(True)

    def is_live(carry):
        return (jnp.max(carry) >= CARRY_FLOOR_B).astype(jnp.int32)

    for t in range(TILES_B):
        diag_block = TILES_B * m + t

        def more_keys(state, diag_block=diag_block):
            step, live = state
            return jnp.logical_and(step < diag_block, live > 0)

        def body(state, t=t, diag_block=diag_block):
            step, _ = state
            j = diag_block - 1 - step
            w, carry = weights(scores(t, j, False), carry_ref[t])
            acc_ref[t] += weighted_values(w, j)
            carry_ref[t] = carry
            return step + 1, is_live(carry)

        lax.while_loop(more_keys, body, (jnp.int32(1), is_live(carry_ref[t])))
        o_ref[0, t * TQ_B:(t + 1) * TQ_B, :] = jnp.where(
            low_lanes, acc_ref[t, 0:TQ_B, :], acc_ref[t, TQ_B:2 * TQ_B, :]).astype(BF16)


def _cumulative_matrix():
    j = np.arange(KB_B)[:, None]
    s = np.arange(KB_B)[None, :]
    strict = (j > s).astype(np.float32)
    return jnp.asarray(np.concatenate([strict, strict], axis=0), BF16)


def _mixer_b(qkvu3):
    b, s, _ = qkvu3.shape
    return pl.pallas_call(
        _mixer_b_kernel,
        grid=(b, N_PAIRS, s // (TILES_B * TQ_B)),
        in_specs=[
            pl.BlockSpec((1, TILES_B * TQ_B, LANES), lambda bi, p, i: (bi, i, 3 * N_PAIRS + p)),
            pl.BlockSpec((1, s, LANES), lambda bi, p, i: (bi, 0, 4 * N_PAIRS + p)),
            pl.BlockSpec((1, s, LANES), lambda bi, p, i: (bi, 0, 5 * N_PAIRS + p)),
            _resident((2 * KB_B, KB_B)),
        ],
        out_specs=pl.BlockSpec((1, TILES_B * TQ_B, LANES), lambda bi, p, i: (bi, i, p)),
        out_shape=jax.ShapeDtypeStruct((b, s, WIDTH), BF16),
        scratch_shapes=[
            pltpu.VMEM((TILES_B, 2 * TQ_B, LANES), BF16),
            pltpu.VMEM((TILES_B, 2 * TQ_B, LANES), F32),
            pltpu.VMEM((TILES_B, 2 * TQ_B, LANES), F32),
        ],
        compiler_params=_compiler_params(("arbitrary", "arbitrary", "arbitrary")),
        name="mixer_b",
    )(qkvu3, qkvu3, qkvu3, _cumulative_matrix())


def _merge_kernel(tiles_per_seq, x_ref, oa_ref, ob_ref, u_ref, uh_ref, gate_ref,
                  wp_ref, ps_ref, wa_ref, wb_ref, wc_ref, wo_ref, o_ref):
    first = (pl.program_id(0) % tiles_per_seq) == 0
    u = u_ref[...].astype(F32)
    halo = jnp.where(first, 0.0, uh_ref[...].astype(F32))
    ext = jnp.concatenate([halo, u], axis=0)
    pos = lax.broadcasted_iota(jnp.int32, (TM_PROJ, 1), 0)

    pooled_groups = []
    for g, win in enumerate(POOL_WINDOWS):
        cols = slice(g * POOL_GROUP_DIM, (g + 1) * POOL_GROUP_DIM)
        a = ext[:, cols]
        step = 1
        while step < win:
            a = a + pltpu.roll(a, step, 0)
            step *= 2
        win_sum = a[HALO_POOL:]
        count = jnp.where(first, jnp.minimum(pos + 1, win), win).astype(F32)
        pooled = win_sum / count - u[:, cols]
        mixed = _dot(pooled.astype(BF16), wp_ref[g])
        pooled_groups.append(mixed * ps_ref[:, cols])
    o_c = jnp.concatenate(pooled_groups, axis=1).astype(BF16)

    merged = (gate_ref[:, 0:D_MODEL].astype(F32) * _dot(oa_ref[...], wa_ref[...])
              + gate_ref[:, D_MODEL:2 * D_MODEL].astype(F32) * _dot(ob_ref[...], wb_ref[...])
              + gate_ref[:, 2 * D_MODEL:3 * D_MODEL].astype(F32) * _dot(o_c, wc_ref[...]))
    o_ref[...] = x_ref[...] + _dot(merged.astype(BF16), wo_ref[...])


def _merge(x2, o_a, o_b, qkvu, gates, w_pool, pool_scale, w_a, w_b, w_c, w_out, seq, layer):
    t = x2.shape[0]
    u_block = QKVU_COLS // WIDTH - 1
    halo_blocks = TM_PROJ // HALO_POOL
    return pl.pallas_call(
        functools.partial(_merge_kernel, seq // TM_PROJ),
        grid=(t // TM_PROJ,),
        in_specs=[
            pl.BlockSpec((TM_PROJ, D_MODEL), lambda i: (i, 0)),
            pl.BlockSpec((TM_PROJ, WIDTH), lambda i: (i, 0)),
            pl.BlockSpec((TM_PROJ, WIDTH), lambda i: (i, 0)),
            pl.BlockSpec((TM_PROJ, WIDTH), lambda i: (i, u_block)),
            pl.BlockSpec((HALO_POOL, WIDTH), lambda i: (jnp.maximum(i * halo_blocks - 1, 0), u_block)),
            pl.BlockSpec((TM_PROJ, GATE_COLS), lambda i: (i, 0)),
            _layer_resident(layer, (len(POOL_WINDOWS), POOL_GROUP_DIM, POOL_GROUP_DIM)),
            _resident((1, WIDTH)),
            _layer_resident(layer, (WIDTH, D_MODEL)),
            _layer_resident(layer, (WIDTH, D_MODEL)),
            _layer_resident(layer, (WIDTH, D_MODEL)),
            _layer_resident(layer, (D_MODEL, D_MODEL)),
        ],
        out_specs=pl.BlockSpec((TM_PROJ, D_MODEL), lambda i: (i, 0)),
        out_shape=jax.ShapeDtypeStruct((t, D_MODEL), F32),
        compiler_params=_compiler_params(("arbitrary",)),
        name="merge",
    )(x2, o_a, o_b, qkvu, qkvu, gates, w_pool, pool_scale, w_a, w_b, w_c, w_out)


def _ffn_kernel(tiles_per_seq, x_ref, g_ref, wu_ref, cw_ref, cb_ref, wd_ref, o_ref,
                halo_ref, h_ref, acc_ref, act_ref):
    first = (pl.program_id(0) % tiles_per_seq) == 0
    groups = TM_FFN // SUBLANES
    x = jnp.swapaxes(x_ref[...].reshape(SUBLANES, groups, D_MODEL), 0, 1).reshape(TM_FFN, D_MODEL)
    h_ref[...] = _rms_norm_rows(x, g_ref[...]).astype(BF16)
    acc_ref[...] = x
    row = lax.broadcasted_iota(jnp.int32, (SUBLANES, 1), 0)

    def up(n0):
        return tuple(_dot(h_ref[...], wu_ref[:, c0:c0 + FF_CHUNK]) for c0 in (n0, D_FF + n0))

    def conv(u, c0):
        cols = slice(c0, c0 + FF_CHUNK)
        prev = jnp.where(first, 0.0, halo_ref[:, cols])
        halo_ref[:, cols] = u[TM_FFN - 2 * SUBLANES:]
        back2 = prev[SUBLANES - 1:SUBLANES]
        back1 = prev[2 * SUBLANES - 1:2 * SUBLANES]
        top1 = jnp.where(row == 0, back1, pltpu.roll(u[TM_FFN - SUBLANES:], 1, 0))
        top2 = jnp.where(row == 0, back2, pltpu.roll(u[TM_FFN - 2 * SUBLANES:TM_FFN - SUBLANES], 1, 0))
        u1 = jnp.concatenate([top1, u[:TM_FFN - SUBLANES]], axis=0)
        u2 = jnp.concatenate([top2, top1, u[:TM_FFN - 2 * SUBLANES]], axis=0)
        return (cb_ref[:, cols] + cw_ref[0:1, cols] * u2 + cw_ref[1:2, cols] * u1
                + cw_ref[2:3, cols] * u)

    chunks = list(range(0, D_FF, FF_CHUNK))
    u_next = up(chunks[0])
    for k, n0 in enumerate(chunks):
        u_gate, u_val = u_next
        if k + 1 < len(chunks):
            u_next = up(chunks[k + 1])
        act = (jax.nn.silu(conv(u_gate, n0)) * conv(u_val, D_FF + n0)).astype(BF16)
        act_ref[:, n0:n0 + FF_CHUNK] = act
    out = acc_ref[...] + _dot(act_ref[...], wd_ref[...])
    o_ref[...] = jnp.swapaxes(out.reshape(groups, SUBLANES, D_MODEL), 0, 1).reshape(TM_FFN, D_MODEL)


def _ffn(x2, gain, w_up, conv_w, conv_b, w_down, seq, layer):
    t = x2.shape[0]
    return pl.pallas_call(
        functools.partial(_ffn_kernel, seq // TM_FFN),
        grid=(t // TM_FFN,),
        in_specs=[
            pl.BlockSpec((TM_FFN, D_MODEL), lambda i: (i, 0)),
            _resident((1, D_MODEL)),
            _layer_resident(layer, (D_MODEL, 2 * D_FF)),
            _resident((3, 2 * D_FF)),
            _resident((1, 2 * D_FF)),
            _layer_resident(layer, (D_FF, D_MODEL)),
        ],
        out_specs=pl.BlockSpec((TM_FFN, D_MODEL), lambda i: (i, 0)),
        out_shape=jax.ShapeDtypeStruct((t, D_MODEL), F32),
        scratch_shapes=[pltpu.VMEM((2 * SUBLANES, 2 * D_FF), F32), pltpu.VMEM((TM_FFN, D_MODEL), BF16),
                        pltpu.VMEM((TM_FFN, D_MODEL), F32), pltpu.VMEM((TM_FFN, D_FF), BF16)],
        compiler_params=_compiler_params(("arbitrary",)),
        name="ffn",
    )(x2, gain, w_up, conv_w, conv_b, w_down)


def kernel(x, norm_mix, w_in, b_gate, q_norm_a, k_norm_a, rel_bias_a, w_pool, pool_scale,
           w_branch_a, w_branch_b, w_branch_c, w_out, norm_ffn, w_up, conv_w, conv_b, w_down):
    b, s, d = x.shape
    assert d == D_MODEL and s % TQ_A == 0 and s % TM_PROJ == 0 and s % (TILES_B * TQ_B) == 0
    depth = w_in.shape[0]
    w_in, w_pool, w_branch_a, w_branch_b, w_branch_c, w_out, w_up, w_down = (
        w.astype(BF16) for w in (w_in, w_pool, w_branch_a, w_branch_b, w_branch_c, w_out, w_up, w_down))
    x2 = x.reshape(b * s, d)
    for l in range(depth):
        gq = jnp.tile(q_norm_a[l], 2)[None, :]
        gk = jnp.tile(k_norm_a[l], 2)[None, :]
        qkvu, gates = _in_proj(x2, norm_mix[l][None, :], w_in, b_gate[l][None, :], gq, gk, l)
        qkvu3 = qkvu.reshape(b, s, QKVU_COLS)
        o_a = _mixer_a(qkvu3, _rel_bias_units(rel_bias_a[l]))
        o_b = _mixer_b(qkvu3)
        x2 = _merge(x2, o_a.reshape(b * s, WIDTH), o_b.reshape(b * s, WIDTH), qkvu, gates,
                    w_pool, pool_scale[l][None, :], w_branch_a, w_branch_b, w_branch_c, w_out, s, l)
        x2 = _ffn(x2, norm_ffn[l][None, :], w_up, conv_w[l], conv_b[l][None, :], w_down, s, l)
    return x2.reshape(b, s, d)
```

```python
import functools
import math

import jax
import jax.numpy as jnp
import numpy as np
from jax import lax
from jax.experimental import pallas as pl
from jax.experimental.pallas import tpu as pltpu

D_MODEL = 1024
CHUNK = 64
N_LEFT_CHUNKS = 8
HEAD_DIM = 64
N_HEADS = 8
N_PAIRS = N_HEADS // 2
WIDTH = N_HEADS * HEAD_DIM
POOL_WINDOWS = (2, 4, 8, 16)
POOL_GROUP_DIM = 128
MAX_REL_DIST = 2 * CHUNK
D_FF = 2816
EPS = 1e-6
MASK_VALUE = -1e30

QKVU_COLS = 7 * WIDTH
GATE_COLS = 3 * D_MODEL

LANES = 128
SUBLANES = 8
VMEM_LIMIT_BYTES = 56 * 1024 * 1024

TM_PROJ = 1024
TN_PROJ = 512
TM_FFN = 1024
LOOKBACK_A = N_LEFT_CHUNKS * CHUNK
TQ_A = 2048
UNIT_A = 2 * CHUNK
BAND_A = (N_LEFT_CHUNKS + 2) * CHUNK
GROUP_A = 4
START_UNITS_A = LOOKBACK_A // UNIT_A
TQ_B = 256
TILES_B = 8
GROUP_B = 8
KB_B = 256
FF_CHUNK = 256
HALO_POOL = 16

LOG2E = 1.4426950408889634
CARRY_FLOOR_B = -152.0

BF16 = jnp.bfloat16
F32 = jnp.float32


def _dot(a, b):
    return jnp.dot(a, b, preferred_element_type=F32)


def _dot_nt(a, b):
    return lax.dot_general(a, b, (((1,), (1,)), ((), ())), preferred_element_type=F32)


def _rms_norm_rows(x, gain):
    ms = jnp.mean(x * x, axis=-1, keepdims=True)
    return x * lax.rsqrt(ms + EPS) * gain


def _compiler_params(semantics):
    return pltpu.CompilerParams(dimension_semantics=semantics,
                                vmem_limit_bytes=VMEM_LIMIT_BYTES)


def _resident(shape):
    zeros = (0,) * len(shape)
    return pl.BlockSpec(shape, lambda *_: zeros, pipeline_mode=pl.Buffered(1))


def _layer_resident(layer, shape):
    index = (layer,) + (0,) * len(shape)
    return pl.BlockSpec((None,) + tuple(shape), lambda *_: index, pipeline_mode=pl.Buffered(1))


def _head_pair_norm(x, gain, low_lanes):
    sq = x * x
    s_low = jnp.sum(jnp.where(low_lanes, sq, 0.0), axis=-1, keepdims=True)
    s_high = jnp.sum(jnp.where(low_lanes, 0.0, sq), axis=-1, keepdims=True)
    ms = jnp.where(low_lanes, s_low, s_high) * (1.0 / HEAD_DIM)
    return x * lax.rsqrt(ms + EPS) * gain


def _low_lanes():
    return lax.broadcasted_iota(jnp.int32, (1, LANES), 1) < HEAD_DIM


def _in_proj_kernel(x_ref, g_ref, w_ref, b_ref, gq_ref, gk_ref, qkvu_ref, gate_ref):
    h = _rms_norm_rows(x_ref[...], g_ref[...]).astype(BF16)
    low_lanes = _low_lanes()
    score_scale = 1.0 / math.sqrt(HEAD_DIM)
    head_norm = {0: gq_ref, WIDTH: gk_ref}
    scaled = {0: score_scale, 3 * WIDTH: score_scale}
    for n0 in range(0, QKVU_COLS, TN_PROJ):
        y = _dot(h, w_ref[:, n0:n0 + TN_PROJ])
        if n0 in head_norm:
            y = jnp.concatenate(
                [_head_pair_norm(y[:, c:c + LANES], head_norm[n0][...], low_lanes)
                 for c in range(0, TN_PROJ, LANES)], axis=1)
        if n0 in scaled:
            y = y * scaled[n0]
        qkvu_ref[:, n0:n0 + TN_PROJ] = y.astype(BF16)
    for n0 in range(0, GATE_COLS, TN_PROJ):
        z = _dot(h, w_ref[:, QKVU_COLS + n0:QKVU_COLS + n0 + TN_PROJ]) + b_ref[:, n0:n0 + TN_PROJ]
        gate_ref[:, n0:n0 + TN_PROJ] = jax.nn.sigmoid(z).astype(BF16)


def _in_proj(x2, gain, w_in, b_gate, gq, gk, layer):
    t = x2.shape[0]
    return pl.pallas_call(
        _in_proj_kernel,
        grid=(t // TM_PROJ,),
        in_specs=[
            pl.BlockSpec((TM_PROJ, D_MODEL), lambda i: (i, 0)),
            _resident((1, D_MODEL)),
            _layer_resident(layer, (D_MODEL, QKVU_COLS + GATE_COLS)),
            _resident((1, GATE_COLS)),
            _resident((1, LANES)),
            _resident((1, LANES)),
        ],
        out_specs=[
            pl.BlockSpec((TM_PROJ, QKVU_COLS), lambda i: (i, 0)),
            pl.BlockSpec((TM_PROJ, GATE_COLS), lambda i: (i, 0)),
        ],
        out_shape=[
            jax.ShapeDtypeStruct((t, QKVU_COLS), BF16),
            jax.ShapeDtypeStruct((t, GATE_COLS), BF16),
        ],
        compiler_params=_compiler_params(("arbitrary",)),
        name="in_proj",
    )(x2, gain, w_in, b_gate, gq, gk)


def _mixer_a_kernel(q_ref, k_ref, v_ref, bias_ref, o_ref, q_scr, kt_scr, v_scr):
    t = pl.program_id(2)
    low_lanes = _low_lanes()
    n_units = TQ_A // UNIT_A

    zero = jnp.zeros((), BF16)
    for u in range(n_units):
        qu = q_ref[0, u * UNIT_A:(u + 1) * UNIT_A, :]
        q_scr[u, 0:UNIT_A, :] = jnp.where(low_lanes, qu, zero)
        q_scr[u, UNIT_A:2 * UNIT_A, :] = jnp.where(low_lanes, zero, qu)

    @pl.when(t == 0)
    def _():
        kt_scr[:, 0:LOOKBACK_A] = jnp.zeros((LANES, LOOKBACK_A), BF16)
        v_scr[0:LOOKBACK_A, :] = jnp.zeros((LOOKBACK_A, LANES), BF16)

    @pl.when(t > 0)
    def _():
        kt_scr[:, 0:LOOKBACK_A] = kt_scr[:, TQ_A:TQ_A + LOOKBACK_A]
        v_scr[0:LOOKBACK_A, :] = v_scr[TQ_A:TQ_A + LOOKBACK_A, :]

    kt_scr[:, LOOKBACK_A:LOOKBACK_A + TQ_A] = k_ref[0].astype(F32).T.astype(BF16)
    v_scr[LOOKBACK_A:LOOKBACK_A + TQ_A, :] = v_ref[0]

    for u0 in range(0, n_units, GROUP_A):
        group = range(u0, min(u0 + GROUP_A, n_units))
        probs, denoms = {}, {}
        for u in group:
            k0 = u * UNIT_A
            variant = jnp.where(t == 0, u + 1, 0) if u < START_UNITS_A else 0
            logits = _dot(q_scr[u], kt_scr[:, k0:k0 + BAND_A]) + bias_ref[0, variant]
            m = jnp.max(logits, axis=-1, keepdims=True)
            p = jnp.exp(logits - m)
            denoms[u] = jnp.sum(p, axis=-1, keepdims=True)
            probs[u] = p.astype(BF16)
        for u in group:
            k0 = u * UNIT_A
            pv = _dot(probs[u], v_scr[k0:k0 + BAND_A, :]) / denoms[u]
            o_ref[0, k0:k0 + UNIT_A, :] = jnp.where(low_lanes, pv[0:UNIT_A], pv[UNIT_A:]).astype(BF16)


def _mixer_a(qkvu3, bias):
    b, s, _ = qkvu3.shape
    blk = (1, TQ_A, LANES)
    return pl.pallas_call(
        _mixer_a_kernel,
        grid=(b, N_PAIRS, s // TQ_A),
        in_specs=[
            pl.BlockSpec(blk, lambda bi, p, t: (bi, t, p)),
            pl.BlockSpec(blk, lambda bi, p, t: (bi, t, N_PAIRS + p)),
            pl.BlockSpec(blk, lambda bi, p, t: (bi, t, 2 * N_PAIRS + p)),
            pl.BlockSpec((1, 1 + START_UNITS_A, 2 * UNIT_A, BAND_A), lambda bi, p, t: (p, 0, 0, 0)),
        ],
        out_specs=pl.BlockSpec(blk, lambda bi, p, t: (bi, t, p)),
        out_shape=jax.ShapeDtypeStruct((b, s, WIDTH), BF16),
        scratch_shapes=[
            pltpu.VMEM((TQ_A // UNIT_A, 2 * UNIT_A, LANES), BF16),
            pltpu.VMEM((LANES, LOOKBACK_A + TQ_A), BF16),
            pltpu.VMEM((LOOKBACK_A + TQ_A, LANES), BF16),
        ],
        compiler_params=_compiler_params(("arbitrary", "arbitrary", "arbitrary")),
        name="mixer_a",
    )(qkvu3, qkvu3, qkvu3, bias)


def _rel_bias_units(rel_bias):
    period = UNIT_A + BAND_A
    d = np.arange(period)
    d = np.where(d < BAND_A, d, d - period)
    idx = np.clip(N_LEFT_CHUNKS * CHUNK - d, -(CHUNK - 1), MAX_REL_DIST) + (CHUNK - 1)
    vec = rel_bias.astype(F32)[:, idx]
    h = rel_bias.shape[0]
    rows = jnp.tile(vec, (1, UNIT_A))[:, :UNIT_A * (period - 1)]
    toeplitz = rows.reshape(h, UNIT_A, period - 1)[:, :, :BAND_A]
    q_chunk = np.arange(UNIT_A)[:, None] // CHUNK
    k_chunk = np.arange(BAND_A)[None, :] // CHUNK
    in_band = (k_chunk >= q_chunk) & (k_chunk <= q_chunk + N_LEFT_CHUNKS)
    table = jnp.where(in_band[None], toeplitz, MASK_VALUE)
    variants = [table]
    for u in range(START_UNITS_A):
        before_start = (np.arange(BAND_A) + u * UNIT_A < LOOKBACK_A)[None, None, :]
        variants.append(jnp.where(before_start, MASK_VALUE, table))
    stacked = jnp.stack(variants, axis=1)
    pairs = stacked.reshape(h // 2, 2, 1 + START_UNITS_A, UNIT_A, BAND_A)
    return jnp.swapaxes(pairs, 1, 2).reshape(h // 2, 1 + START_UNITS_A, 2 * UNIT_A, BAND_A)


def _mixer_b_kernel(q_ref, k_ref, v_ref, cum_ref, o_ref, q_scr, acc_ref, carry_ref, top_ref):
    m = pl.program_id(2)
    low_lanes = _low_lanes()
    zero = jnp.zeros((), BF16)
    for tile in range(TILES_B):
        q = q_ref[0, tile * TQ_B:(tile + 1) * TQ_B, :]
        q_scr[tile, 0:TQ_B, :] = jnp.where(low_lanes, q, zero)
        q_scr[tile, TQ_B:2 * TQ_B, :] = jnp.where(low_lanes, zero, q)

    def block_start(j):
        return pl.multiple_of(j * KB_B, KB_B)

    def scores(tile, j, diagonal):
        neg_z = _dot_nt(q_scr[tile], k_ref[0, pl.ds(block_start(j), KB_B), :]) * (-LOG2E)
        soft = jnp.log2(1.0 + jnp.exp2(-jnp.abs(neg_z)))
        log_keep = jnp.minimum(neg_z, 0.0) - soft
        log_beta = log_keep - neg_z
        before = None
        if diagonal:
            row = lax.broadcasted_iota(jnp.int32, (2 * TQ_B, KB_B), 0)
            col = lax.broadcasted_iota(jnp.int32, (2 * TQ_B, KB_B), 1)
            before = col < jnp.where(row >= TQ_B, row - TQ_B, row)
            log_keep = jnp.where(before, log_keep, 0.0)
        return log_keep.astype(BF16), log_beta, log_keep[:, 0:1], before

    def weights(scored, carry):
        keep, log_beta, first_keep, before = scored
        tail = _dot(keep, cum_ref[...])
        exponent = log_beta + tail
        if carry is not None:
            exponent = exponent + jnp.concatenate([carry] * (KB_B // LANES), axis=1)
        w = jnp.exp2(exponent)
        if before is not None:
            w = jnp.where(before, w, 0.0)
        block_total = jnp.broadcast_to(tail[:, 0:1] + first_keep, (2 * TQ_B, LANES))
        return w.astype(BF16), (block_total if carry is None else carry + block_total)

    def weighted_values(w, j):
        return _dot(w, v_ref[0, pl.ds(block_start(j), KB_B), :])

    def first_visits(tiles, tiles_with_previous):
        diag_block = {t: TILES_B * m + t for t in tiles}
        scored_diag = {t: scores(t, diag_block[t], True) for t in tiles}
        scored_prev = {t: scores(t, diag_block[t] - 1, False) for t in tiles_with_previous}
        w_diag, w_prev, carry = {}, {}, {}
        for t in tiles:
            w_diag[t], carry[t] = weights(scored_diag[t], None)
        for t in tiles_with_previous:
            w_prev[t], carry[t] = weights(scored_prev[t], carry[t])
        for t in tiles:
            pv = weighted_values(w_diag[t], diag_block[t])
            if t in tiles_with_previous:
                pv = pv + weighted_values(w_prev[t], diag_block[t] - 1)
            acc_ref[t] = pv
            carry_ref[t] = carry[t]
            top_ref[t] = jnp.max(carry[t])

    def first_visits_grouped(first_tile_has_previous):
        for t0 in range(0, TILES_B, GROUP_B):
            tiles = list(range(t0, min(t0 + GROUP_B, TILES_B)))
            first_visits(tiles, [t for t in tiles if t > 0 or first_tile_has_previous])

    @pl.when(m == 0)
    def _():
        first_visits_grouped(False)

    @pl.when(m > 0)
    def _():
        first_visits_grouped(True)

    def is_live(top):
        return (top >= CARRY_FLOOR_B).astype(jnp.int32)

    any_live = jnp.int32(0)
    for t in range(TILES_B):
        has_more_blocks = (TILES_B * m + t >= 2).astype(jnp.int32)
        any_live = jnp.maximum(any_live, has_more_blocks * is_live(top_ref[t]))

    @pl.when(any_live > 0)
    def _():
        for t in range(TILES_B):
            diag_block = TILES_B * m + t

            def more_keys(state, diag_block=diag_block):
                step, live = state
                return jnp.logical_and(step < diag_block, live > 0)

            def body(state, t=t, diag_block=diag_block):
                step, _ = state
                j = diag_block - 1 - step
                w, carry = weights(scores(t, j, False), carry_ref[t])
                acc_ref[t] += weighted_values(w, j)
                carry_ref[t] = carry
                return step + 1, is_live(jnp.max(carry))

            lax.while_loop(more_keys, body, (jnp.int32(1), is_live(top_ref[t])))

    for t in range(TILES_B):
        o_ref[0, t * TQ_B:(t + 1) * TQ_B, :] = jnp.where(
            low_lanes, acc_ref[t, 0:TQ_B, :], acc_ref[t, TQ_B:2 * TQ_B, :]).astype(BF16)


def _cumulative_matrix():
    j = np.arange(KB_B)[:, None]
    s = np.arange(KB_B)[None, :]
    return jnp.asarray((j > s).astype(np.float32), BF16)


def _mixer_b(qkvu3):
    b, s, _ = qkvu3.shape
    return pl.pallas_call(
        _mixer_b_kernel,
        grid=(b, N_PAIRS, s // (TILES_B * TQ_B)),
        in_specs=[
            pl.BlockSpec((1, TILES_B * TQ_B, LANES), lambda bi, p, i: (bi, i, 3 * N_PAIRS + p)),
            pl.BlockSpec((1, s, LANES), lambda bi, p, i: (bi, 0, 4 * N_PAIRS + p)),
            pl.BlockSpec((1, s, LANES), lambda bi, p, i: (bi, 0, 5 * N_PAIRS + p)),
            _resident((KB_B, KB_B)),
        ],
        out_specs=pl.BlockSpec((1, TILES_B * TQ_B, LANES), lambda bi, p, i: (bi, i, p)),
        out_shape=jax.ShapeDtypeStruct((b, s, WIDTH), BF16),
        scratch_shapes=[
            pltpu.VMEM((TILES_B, 2 * TQ_B, LANES), BF16),
            pltpu.VMEM((TILES_B, 2 * TQ_B, LANES), F32),
            pltpu.VMEM((TILES_B, 2 * TQ_B, LANES), F32),
            pltpu.SMEM((TILES_B,), F32),
        ],
        compiler_params=_compiler_params(("arbitrary", "arbitrary", "arbitrary")),
        name="mixer_b",
    )(qkvu3, qkvu3, qkvu3, _cumulative_matrix())


def _merge_kernel(tiles_per_seq, x_ref, oa_ref, ob_ref, u_ref, uh_ref, gate_ref,
                  wp_ref, ps_ref, wa_ref, wb_ref, wc_ref, wo_ref, o_ref):
    first = (pl.program_id(0) % tiles_per_seq) == 0
    u = u_ref[...].astype(F32)
    halo = jnp.where(first, 0.0, uh_ref[...].astype(F32))
    ext = jnp.concatenate([halo, u], axis=0)
    pos = lax.broadcasted_iota(jnp.int32, (TM_PROJ, 1), 0)

    pooled_groups = []
    for g, win in enumerate(POOL_WINDOWS):
        cols = slice(g * POOL_GROUP_DIM, (g + 1) * POOL_GROUP_DIM)
        a = ext[:, cols]
        step = 1
        while step < win:
            a = a + pltpu.roll(a, step, 0)
            step *= 2
        win_sum = a[HALO_POOL:]
        count = jnp.where(first, jnp.minimum(pos + 1, win), win).astype(F32)
        pooled = win_sum / count - u[:, cols]
        mixed = _dot(pooled.astype(BF16), wp_ref[g])
        pooled_groups.append(mixed * ps_ref[:, cols])
    o_c = jnp.concatenate(pooled_groups, axis=1).astype(BF16)

    merged = (gate_ref[:, 0:D_MODEL].astype(F32) * _dot(oa_ref[...], wa_ref[...])
              + gate_ref[:, D_MODEL:2 * D_MODEL].astype(F32) * _dot(ob_ref[...], wb_ref[...])
              + gate_ref[:, 2 * D_MODEL:3 * D_MODEL].astype(F32) * _dot(o_c, wc_ref[...]))
    o_ref[...] = x_ref[...] + _dot(merged.astype(BF16), wo_ref[...])


def _merge(x2, o_a, o_b, qkvu, gates, w_pool, pool_scale, w_a, w_b, w_c, w_out, seq, layer):
    t = x2.shape[0]
    u_block = QKVU_COLS // WIDTH - 1
    halo_blocks = TM_PROJ // HALO_POOL
    return pl.pallas_call(
        functools.partial(_merge_kernel, seq // TM_PROJ),
        grid=(t // TM_PROJ,),
        in_specs=[
            pl.BlockSpec((TM_PROJ, D_MODEL), lambda i: (i, 0)),
            pl.BlockSpec((TM_PROJ, WIDTH), lambda i: (i, 0)),
            pl.BlockSpec((TM_PROJ, WIDTH), lambda i: (i, 0)),
            pl.BlockSpec((TM_PROJ, WIDTH), lambda i: (i, u_block)),
            pl.BlockSpec((HALO_POOL, WIDTH), lambda i: (jnp.maximum(i * halo_blocks - 1, 0), u_block)),
            pl.BlockSpec((TM_PROJ, GATE_COLS), lambda i: (i, 0)),
            _layer_resident(layer, (len(POOL_WINDOWS), POOL_GROUP_DIM, POOL_GROUP_DIM)),
            _resident((1, WIDTH)),
            _layer_resident(layer, (WIDTH, D_MODEL)),
            _layer_resident(layer, (WIDTH, D_MODEL)),
            _layer_resident(layer, (WIDTH, D_MODEL)),
            _layer_resident(layer, (D_MODEL, D_MODEL)),
        ],
        out_specs=pl.BlockSpec((TM_PROJ, D_MODEL), lambda i: (i, 0)),
        out_shape=jax.ShapeDtypeStruct((t, D_MODEL), F32),
        compiler_params=_compiler_params(("arbitrary",)),
        name="merge",
    )(x2, o_a, o_b, qkvu, qkvu, gates, w_pool, pool_scale, w_a, w_b, w_c, w_out)


def _ffn_kernel(tiles_per_seq, x_ref, g_ref, wu_ref, cw_ref, cb_ref, wd_ref, o_ref,
                halo_ref, h_ref, acc_ref, act_ref):
    first = (pl.program_id(0) % tiles_per_seq) == 0
    groups = TM_FFN // SUBLANES
    x = jnp.swapaxes(x_ref[...].reshape(SUBLANES, groups, D_MODEL), 0, 1).reshape(TM_FFN, D_MODEL)
    h_ref[...] = _rms_norm_rows(x, g_ref[...]).astype(BF16)
    acc_ref[...] = x
    row = lax.broadcasted_iota(jnp.int32, (SUBLANES, 1), 0)

    def up(n0):
        return tuple(_dot(h_ref[...], wu_ref[:, c0:c0 + FF_CHUNK]) for c0 in (n0, D_FF + n0))

    def conv(u, c0):
        cols = slice(c0, c0 + FF_CHUNK)
        prev = jnp.where(first, 0.0, halo_ref[:, cols])
        halo_ref[:, cols] = u[TM_FFN - 2 * SUBLANES:]
        back2 = prev[SUBLANES - 1:SUBLANES]
        back1 = prev[2 * SUBLANES - 1:2 * SUBLANES]
        top1 = jnp.where(row == 0, back1, pltpu.roll(u[TM_FFN - SUBLANES:], 1, 0))
        top2 = jnp.where(row == 0, back2, pltpu.roll(u[TM_FFN - 2 * SUBLANES:TM_FFN - SUBLANES], 1, 0))
        u1 = jnp.concatenate([top1, u[:TM_FFN - SUBLANES]], axis=0)
        u2 = jnp.concatenate([top2, top1, u[:TM_FFN - 2 * SUBLANES]], axis=0)
        return (cb_ref[:, cols] + cw_ref[0:1, cols] * u2 + cw_ref[1:2, cols] * u1
                + cw_ref[2:3, cols] * u)

    chunks = list(range(0, D_FF, FF_CHUNK))
    u_next = up(chunks[0])
    for k, n0 in enumerate(chunks):
        u_gate, u_val = u_next
        if k + 1 < len(chunks):
            u_next = up(chunks[k + 1])
        act = (jax.nn.silu(conv(u_gate, n0)) * conv(u_val, D_FF + n0)).astype(BF16)
        act_ref[:, n0:n0 + FF_CHUNK] = act
    out = acc_ref[...] + _dot(act_ref[...], wd_ref[...])
    o_ref[...] = jnp.swapaxes(out.reshape(groups, SUBLANES, D_MODEL), 0, 1).reshape(TM_FFN, D_MODEL)


def _ffn(x2, gain, w_up, conv_w, conv_b, w_down, seq, layer):
    t = x2.shape[0]
    return pl.pallas_call(
        functools.partial(_ffn_kernel, seq // TM_FFN),
        grid=(t // TM_FFN,),
        in_specs=[
            pl.BlockSpec((TM_FFN, D_MODEL), lambda i: (i, 0)),
            _resident((1, D_MODEL)),
            _layer_resident(layer, (D_MODEL, 2 * D_FF)),
            _resident((3, 2 * D_FF)),
            _resident((1, 2 * D_FF)),
            _layer_resident(layer, (D_FF, D_MODEL)),
        ],
        out_specs=pl.BlockSpec((TM_FFN, D_MODEL), lambda i: (i, 0)),
        out_shape=jax.ShapeDtypeStruct((t, D_MODEL), F32),
        scratch_shapes=[pltpu.VMEM((2 * SUBLANES, 2 * D_FF), F32), pltpu.VMEM((TM_FFN, D_MODEL), BF16),
                        pltpu.VMEM((TM_FFN, D_MODEL), F32), pltpu.VMEM((TM_FFN, D_FF), BF16)],
        compiler_params=_compiler_params(("arbitrary",)),
        name="ffn",
    )(x2, gain, w_up, conv_w, conv_b, w_down)


def kernel(x, norm_mix, w_in, b_gate, q_norm_a, k_norm_a, rel_bias_a, w_pool, pool_scale,
           w_branch_a, w_branch_b, w_branch_c, w_out, norm_ffn, w_up, conv_w, conv_b, w_down):
    b, s, d = x.shape
    assert d == D_MODEL and s % TQ_A == 0 and s % TM_PROJ == 0 and s % (TILES_B * TQ_B) == 0
    depth = w_in.shape[0]
    w_in, w_pool, w_branch_a, w_branch_b, w_branch_c, w_out, w_up, w_down = (
        w.astype(BF16) for w in (w_in, w_pool, w_branch_a, w_branch_b, w_branch_c, w_out, w_up, w_down))
    x2 = x.reshape(b * s, d)
    for l in range(depth):
        gq = jnp.tile(q_norm_a[l], 2)[None, :]
        gk = jnp.tile(k_norm_a[l], 2)[None, :]
        qkvu, gates = _in_proj(x2, norm_mix[l][None, :], w_in, b_gate[l][None, :], gq, gk, l)
        qkvu3 = qkvu.reshape(b, s, QKVU_COLS)
        o_a = _mixer_a(qkvu3, _rel_bias_units(rel_bias_a[l]))
        o_b = _mixer_b(qkvu3)
        x2 = _merge(x2, o_a.reshape(b * s, WIDTH), o_b.reshape(b * s, WIDTH), qkvu, gates,
                    w_pool, pool_scale[l][None, :], w_branch_a, w_branch_b, w_branch_c, w_out, s, l)
        x2 = _ffn(x2, norm_ffn[l][None, :], w_up, conv_w[l], conv_b[l][None, :], w_down, s, l)
    return x2.reshape(b, s, d)
```

```python
import functools
import math

import jax
import jax.numpy as jnp
import numpy as np
from jax import lax
from jax.experimental import pallas as pl
from jax.experimental.pallas import tpu as pltpu

D_MODEL = 1024
CHUNK = 64
N_LEFT_CHUNKS = 8
HEAD_DIM = 64
N_HEADS = 8
N_PAIRS = N_HEADS // 2
WIDTH = N_HEADS * HEAD_DIM
POOL_WINDOWS = (2, 4, 8, 16)
POOL_GROUP_DIM = 128
MAX_REL_DIST = 2 * CHUNK
D_FF = 2816
EPS = 1e-6
MASK_VALUE = -1e30

QKVU_COLS = 7 * WIDTH
GATE_COLS = 3 * D_MODEL

LANES = 128
SUBLANES = 8
VMEM_LIMIT_BYTES = 56 * 1024 * 1024

TM_PROJ = 1024
TN_PROJ = 512
TM_FFN = 1024
LOOKBACK_A = N_LEFT_CHUNKS * CHUNK
TQ_A = 4096
UNIT_A = 2 * CHUNK
BAND_A = (N_LEFT_CHUNKS + 2) * CHUNK
GROUP_A = 2
START_UNITS_A = LOOKBACK_A // UNIT_A
TQ_B = 256
TILES_B = 8
GROUP_B = 8
KB_B = 256
FF_CHUNK = 256
HALO_POOL = 16

LOG2E = 1.4426950408889634
CARRY_FLOOR_B = -152.0

BF16 = jnp.bfloat16
F32 = jnp.float32


def _dot(a, b):
    return jnp.dot(a, b, preferred_element_type=F32)


def _dot_nt(a, b):
    return lax.dot_general(a, b, (((1,), (1,)), ((), ())), preferred_element_type=F32)


def _rms_norm_rows(x, gain):
    ms = jnp.mean(x * x, axis=-1, keepdims=True)
    return x * lax.rsqrt(ms + EPS) * gain


def _compiler_params(semantics):
    return pltpu.CompilerParams(dimension_semantics=semantics,
                                vmem_limit_bytes=VMEM_LIMIT_BYTES)


def _resident(shape):
    zeros = (0,) * len(shape)
    return pl.BlockSpec(shape, lambda *_: zeros, pipeline_mode=pl.Buffered(1))


def _layer_resident(layer, shape):
    index = (layer,) + (0,) * len(shape)
    return pl.BlockSpec((None,) + tuple(shape), lambda *_: index, pipeline_mode=pl.Buffered(1))


def _head_pair_norm(x, gain, low_lanes):
    sq = x * x
    s_low = jnp.sum(jnp.where(low_lanes, sq, 0.0), axis=-1, keepdims=True)
    s_high = jnp.sum(jnp.where(low_lanes, 0.0, sq), axis=-1, keepdims=True)
    ms = jnp.where(low_lanes, s_low, s_high) * (1.0 / HEAD_DIM)
    return x * lax.rsqrt(ms + EPS) * gain


def _low_lanes():
    return lax.broadcasted_iota(jnp.int32, (1, LANES), 1) < HEAD_DIM


def _in_proj_kernel(x_ref, g_ref, w_ref, b_ref, gq_ref, gk_ref, qkvu_ref, gate_ref):
    h = _rms_norm_rows(x_ref[...], g_ref[...]).astype(BF16)
    low_lanes = _low_lanes()
    score_scale = 1.0 / math.sqrt(HEAD_DIM)
    head_norm = {0: gq_ref, WIDTH: gk_ref}
    scaled = {0: score_scale, 3 * WIDTH: score_scale}
    for n0 in range(0, QKVU_COLS, TN_PROJ):
        y = _dot(h, w_ref[:, n0:n0 + TN_PROJ])
        if n0 in head_norm:
            y = jnp.concatenate(
                [_head_pair_norm(y[:, c:c + LANES], head_norm[n0][...], low_lanes)
                 for c in range(0, TN_PROJ, LANES)], axis=1)
        if n0 in scaled:
            y = y * scaled[n0]
        qkvu_ref[:, n0:n0 + TN_PROJ] = y.astype(BF16)
    for n0 in range(0, GATE_COLS, TN_PROJ):
        z = _dot(h, w_ref[:, QKVU_COLS + n0:QKVU_COLS + n0 + TN_PROJ]) + b_ref[:, n0:n0 + TN_PROJ]
        gate_ref[:, n0:n0 + TN_PROJ] = jax.nn.sigmoid(z).astype(BF16)


def _in_proj(x2, gain, w_in, b_gate, gq, gk, layer):
    t = x2.shape[0]
    return pl.pallas_call(
        _in_proj_kernel,
        grid=(t // TM_PROJ,),
        in_specs=[
            pl.BlockSpec((TM_PROJ, D_MODEL), lambda i: (i, 0)),
            _resident((1, D_MODEL)),
            _layer_resident(layer, (D_MODEL, QKVU_COLS + GATE_COLS)),
            _resident((1, GATE_COLS)),
            _resident((1, LANES)),
            _resident((1, LANES)),
        ],
        out_specs=[
            pl.BlockSpec((TM_PROJ, QKVU_COLS), lambda i: (i, 0)),
            pl.BlockSpec((TM_PROJ, GATE_COLS), lambda i: (i, 0)),
        ],
        out_shape=[
            jax.ShapeDtypeStruct((t, QKVU_COLS), BF16),
            jax.ShapeDtypeStruct((t, GATE_COLS), BF16),
        ],
        compiler_params=_compiler_params(("arbitrary",)),
        name="in_proj",
    )(x2, gain, w_in, b_gate, gq, gk)


def _mixer_a_kernel(q_ref, k_ref, v_ref, bias_ref, o_ref, q_scr, kt_scr, v_scr):
    t = pl.program_id(2)
    low_lanes = _low_lanes()
    n_units = TQ_A // UNIT_A

    zero = jnp.zeros((), BF16)
    for u in range(n_units):
        qu = q_ref[0, u * UNIT_A:(u + 1) * UNIT_A, :]
        q_scr[u, 0:UNIT_A, :] = jnp.where(low_lanes, qu, zero)
        q_scr[u, UNIT_A:2 * UNIT_A, :] = jnp.where(low_lanes, zero, qu)

    @pl.when(t == 0)
    def _():
        kt_scr[:, 0:LOOKBACK_A] = jnp.zeros((LANES, LOOKBACK_A), BF16)
        v_scr[0:LOOKBACK_A, :] = jnp.zeros((LOOKBACK_A, LANES), BF16)

    @pl.when(t > 0)
    def _():
        kt_scr[:, 0:LOOKBACK_A] = kt_scr[:, TQ_A:TQ_A + LOOKBACK_A]
        v_scr[0:LOOKBACK_A, :] = v_scr[TQ_A:TQ_A + LOOKBACK_A, :]

    kt_scr[:, LOOKBACK_A:LOOKBACK_A + TQ_A] = k_ref[0].astype(F32).T.astype(BF16)
    v_scr[LOOKBACK_A:LOOKBACK_A + TQ_A, :] = v_ref[0]

    for u0 in range(0, n_units, GROUP_A):
        group = range(u0, min(u0 + GROUP_A, n_units))
        probs, denoms = {}, {}
        for u in group:
            k0 = u * UNIT_A
            variant = jnp.where(t == 0, u + 1, 0) if u < START_UNITS_A else 0
            logits = _dot(q_scr[u], kt_scr[:, k0:k0 + BAND_A]) + bias_ref[0, variant]
            m = jnp.max(logits, axis=-1, keepdims=True)
            p = jnp.exp(logits - m)
            denoms[u] = jnp.sum(p, axis=-1, keepdims=True)
            probs[u] = p.astype(BF16)
        for u in group:
            k0 = u * UNIT_A
            pv = _dot(probs[u], v_scr[k0:k0 + BAND_A, :]) / denoms[u]
            o_ref[0, k0:k0 + UNIT_A, :] = jnp.where(low_lanes, pv[0:UNIT_A], pv[UNIT_A:]).astype(BF16)


def _mixer_a(qkvu3, bias):
    b, s, _ = qkvu3.shape
    blk = (1, TQ_A, LANES)
    return pl.pallas_call(
        _mixer_a_kernel,
        grid=(b, N_PAIRS, s // TQ_A),
        in_specs=[
            pl.BlockSpec(blk, lambda bi, p, t: (bi, t, p)),
            pl.BlockSpec(blk, lambda bi, p, t: (bi, t, N_PAIRS + p)),
            pl.BlockSpec(blk, lambda bi, p, t: (bi, t, 2 * N_PAIRS + p)),
            pl.BlockSpec((1, 1 + START_UNITS_A, 2 * UNIT_A, BAND_A), lambda bi, p, t: (p, 0, 0, 0)),
        ],
        out_specs=pl.BlockSpec(blk, lambda bi, p, t: (bi, t, p)),
        out_shape=jax.ShapeDtypeStruct((b, s, WIDTH), BF16),
        scratch_shapes=[
            pltpu.VMEM((TQ_A // UNIT_A, 2 * UNIT_A, LANES), BF16),
            pltpu.VMEM((LANES, LOOKBACK_A + TQ_A), BF16),
            pltpu.VMEM((LOOKBACK_A + TQ_A, LANES), BF16),
        ],
        compiler_params=_compiler_params(("arbitrary", "arbitrary", "arbitrary")),
        name="mixer_a",
    )(qkvu3, qkvu3, qkvu3, bias)


def _rel_bias_units(rel_bias):
    period = UNIT_A + BAND_A
    d = np.arange(period)
    d = np.where(d < BAND_A, d, d - period)
    idx = np.clip(N_LEFT_CHUNKS * CHUNK - d, -(CHUNK - 1), MAX_REL_DIST) + (CHUNK - 1)
    vec = rel_bias.astype(F32)[:, idx]
    h = rel_bias.shape[0]
    rows = jnp.tile(vec, (1, UNIT_A))[:, :UNIT_A * (period - 1)]
    toeplitz = rows.reshape(h, UNIT_A, period - 1)[:, :, :BAND_A]
    q_chunk = np.arange(UNIT_A)[:, None] // CHUNK
    k_chunk = np.arange(BAND_A)[None, :] // CHUNK
    in_band = (k_chunk >= q_chunk) & (k_chunk <= q_chunk + N_LEFT_CHUNKS)
    table = jnp.where(in_band[None], toeplitz, MASK_VALUE)
    variants = [table]
    for u in range(START_UNITS_A):
        before_start = (np.arange(BAND_A) + u * UNIT_A < LOOKBACK_A)[None, None, :]
        variants.append(jnp.where(before_start, MASK_VALUE, table))
    stacked = jnp.stack(variants, axis=1)
    pairs = stacked.reshape(h // 2, 2, 1 + START_UNITS_A, UNIT_A, BAND_A)
    return jnp.swapaxes(pairs, 1, 2).reshape(h // 2, 1 + START_UNITS_A, 2 * UNIT_A, BAND_A)


def _mixer_b_kernel(q_ref, k_ref, v_ref, cum_ref, o_ref, q_scr, acc_ref, carry_ref, top_ref):
    m = pl.program_id(2)
    low_lanes = _low_lanes()
    zero = jnp.zeros((), BF16)
    for tile in range(TILES_B):
        q = q_ref[0, tile * TQ_B:(tile + 1) * TQ_B, :]
        q_scr[tile, 0:TQ_B, :] = jnp.where(low_lanes, q, zero)
        q_scr[tile, TQ_B:2 * TQ_B, :] = jnp.where(low_lanes, zero, q)

    def block_start(j):
        return pl.multiple_of(j * KB_B, KB_B)

    def scores(tile, j, diagonal):
        neg_z = _dot_nt(q_scr[tile], k_ref[0, pl.ds(block_start(j), KB_B), :]) * (-LOG2E)
        soft = jnp.log2(1.0 + jnp.exp2(-jnp.abs(neg_z)))
        log_keep = jnp.minimum(neg_z, 0.0) - soft
        log_beta = log_keep - neg_z
        before = None
        if diagonal:
            row = lax.broadcasted_iota(jnp.int32, (2 * TQ_B, KB_B), 0)
            col = lax.broadcasted_iota(jnp.int32, (2 * TQ_B, KB_B), 1)
            before = col < jnp.where(row >= TQ_B, row - TQ_B, row)
            log_keep = jnp.where(before, log_keep, 0.0)
        return log_keep.astype(BF16), log_beta, log_keep[:, 0:1], before

    def weights(scored, carry):
        keep, log_beta, first_keep, before = scored
        tail = _dot(keep, cum_ref[...])
        exponent = log_beta + tail
        if carry is not None:
            exponent = exponent + jnp.concatenate([carry] * (KB_B // LANES), axis=1)
        w = jnp.exp2(exponent)
        if before is not None:
            w = jnp.where(before, w, 0.0)
        block_total = jnp.broadcast_to(tail[:, 0:1] + first_keep, (2 * TQ_B, LANES))
        return w.astype(BF16), (block_total if carry is None else carry + block_total)

    def weighted_values(w, j):
        return _dot(w, v_ref[0, pl.ds(block_start(j), KB_B), :])

    def first_visits(tiles, tiles_with_previous):
        diag_block = {t: TILES_B * m + t for t in tiles}
        scored_diag = {t: scores(t, diag_block[t], True) for t in tiles}
        scored_prev = {t: scores(t, diag_block[t] - 1, False) for t in tiles_with_previous}
        w_diag, w_prev, carry = {}, {}, {}
        for t in tiles:
            w_diag[t], carry[t] = weights(scored_diag[t], None)
        for t in tiles_with_previous:
            w_prev[t], carry[t] = weights(scored_prev[t], carry[t])
        for t in tiles:
            pv = weighted_values(w_diag[t], diag_block[t])
            if t in tiles_with_previous:
                pv = pv + weighted_values(w_prev[t], diag_block[t] - 1)
            acc_ref[t] = pv
            carry_ref[t] = carry[t]
            top_ref[t] = jnp.max(carry[t])

    def first_visits_grouped(first_tile_has_previous):
        for t0 in range(0, TILES_B, GROUP_B):
            tiles = list(range(t0, min(t0 + GROUP_B, TILES_B)))
            first_visits(tiles, [t for t in tiles if t > 0 or first_tile_has_previous])

    @pl.when(m == 0)
    def _():
        first_visits_grouped(False)

    @pl.when(m > 0)
    def _():
        first_visits_grouped(True)

    def is_live(top):
        return (top >= CARRY_FLOOR_B).astype(jnp.int32)

    any_live = jnp.int32(0)
    for t in range(TILES_B):
        has_more_blocks = (TILES_B * m + t >= 2).astype(jnp.int32)
        any_live = jnp.maximum(any_live, has_more_blocks * is_live(top_ref[t]))

    @pl.when(any_live > 0)
    def _():
        for t in range(TILES_B):
            diag_block = TILES_B * m + t

            def more_keys(state, diag_block=diag_block):
                step, live = state
                return jnp.logical_and(step < diag_block, live > 0)

            def body(state, t=t, diag_block=diag_block):
                step, _ = state
                j = diag_block - 1 - step
                w, carry = weights(scores(t, j, False), carry_ref[t])
                acc_ref[t] += weighted_values(w, j)
                carry_ref[t] = carry
                return step + 1, is_live(jnp.max(carry))

            lax.while_loop(more_keys, body, (jnp.int32(1), is_live(top_ref[t])))

    for t in range(TILES_B):
        o_ref[0, t * TQ_B:(t + 1) * TQ_B, :] = jnp.where(
            low_lanes, acc_ref[t, 0:TQ_B, :], acc_ref[t, TQ_B:2 * TQ_B, :]).astype(BF16)


def _cumulative_matrix():
    j = np.arange(KB_B)[:, None]
    s = np.arange(KB_B)[None, :]
    return jnp.asarray((j > s).astype(np.float32), BF16)


def _mixer_b(qkvu3):
    b, s, _ = qkvu3.shape
    return pl.pallas_call(
        _mixer_b_kernel,
        grid=(b, N_PAIRS, s // (TILES_B * TQ_B)),
        in_specs=[
            pl.BlockSpec((1, TILES_B * TQ_B, LANES), lambda bi, p, i: (bi, i, 3 * N_PAIRS + p)),
            pl.BlockSpec((1, s, LANES), lambda bi, p, i: (bi, 0, 4 * N_PAIRS + p)),
            pl.BlockSpec((1, s, LANES), lambda bi, p, i: (bi, 0, 5 * N_PAIRS + p)),
            _resident((KB_B, KB_B)),
        ],
        out_specs=pl.BlockSpec((1, TILES_B * TQ_B, LANES), lambda bi, p, i: (bi, i, p)),
        out_shape=jax.ShapeDtypeStruct((b, s, WIDTH), BF16),
        scratch_shapes=[
            pltpu.VMEM((TILES_B, 2 * TQ_B, LANES), BF16),
            pltpu.VMEM((TILES_B, 2 * TQ_B, LANES), F32),
            pltpu.VMEM((TILES_B, 2 * TQ_B, LANES), F32),
            pltpu.SMEM((TILES_B,), F32),
        ],
        compiler_params=_compiler_params(("arbitrary", "arbitrary", "arbitrary")),
        name="mixer_b",
    )(qkvu3, qkvu3, qkvu3, _cumulative_matrix())


def _merge_kernel(tiles_per_seq, x_ref, oa_ref, ob_ref, u_ref, uh_ref, gate_ref,
                  wp_ref, ps_ref, wa_ref, wb_ref, wc_ref, wo_ref, o_ref):
    first = (pl.program_id(0) % tiles_per_seq) == 0
    u = u_ref[...].astype(F32)
    halo = jnp.where(first, 0.0, uh_ref[...].astype(F32))
    ext = jnp.concatenate([halo, u], axis=0)
    pos = lax.broadcasted_iota(jnp.int32, (TM_PROJ, 1), 0)

    pooled_groups = []
    for g, win in enumerate(POOL_WINDOWS):
        cols = slice(g * POOL_GROUP_DIM, (g + 1) * POOL_GROUP_DIM)
        a = ext[:, cols]
        step = 1
        while step < win:
            a = a + pltpu.roll(a, step, 0)
            step *= 2
        win_sum = a[HALO_POOL:]
        count = jnp.where(first, jnp.minimum(pos + 1, win), win).astype(F32)
        pooled = win_sum / count - u[:, cols]
        mixed = _dot(pooled.astype(BF16), wp_ref[g])
        pooled_groups.append(mixed * ps_ref[:, cols])
    o_c = jnp.concatenate(pooled_groups, axis=1).astype(BF16)

    merged = (gate_ref[:, 0:D_MODEL].astype(F32) * _dot(oa_ref[...], wa_ref[...])
              + gate_ref[:, D_MODEL:2 * D_MODEL].astype(F32) * _dot(ob_ref[...], wb_ref[...])
              + gate_ref[:, 2 * D_MODEL:3 * D_MODEL].astype(F32) * _dot(o_c, wc_ref[...]))
    o_ref[...] = x_ref[...] + _dot(merged.astype(BF16), wo_ref[...])


def _merge(x2, o_a, o_b, qkvu, gates, w_pool, pool_scale, w_a, w_b, w_c, w_out, seq, layer):
    t = x2.shape[0]
    u_block = QKVU_COLS // WIDTH - 1
    halo_blocks = TM_PROJ // HALO_POOL
    return pl.pallas_call(
        functools.partial(_merge_kernel, seq // TM_PROJ),
        grid=(t // TM_PROJ,),
        in_specs=[
            pl.BlockSpec((TM_PROJ, D_MODEL), lambda i: (i, 0)),
            pl.BlockSpec((TM_PROJ, WIDTH), lambda i: (i, 0)),
            pl.BlockSpec((TM_PROJ, WIDTH), lambda i: (i, 0)),
            pl.BlockSpec((TM_PROJ, WIDTH), lambda i: (i, u_block)),
            pl.BlockSpec((HALO_POOL, WIDTH), lambda i: (jnp.maximum(i * halo_blocks - 1, 0), u_block)),
            pl.BlockSpec((TM_PROJ, GATE_COLS), lambda i: (i, 0)),
            _layer_resident(layer, (len(POOL_WINDOWS), POOL_GROUP_DIM, POOL_GROUP_DIM)),
            _resident((1, WIDTH)),
            _layer_resident(layer, (WIDTH, D_MODEL)),
            _layer_resident(layer, (WIDTH, D_MODEL)),
            _layer_resident(layer, (WIDTH, D_MODEL)),
            _layer_resident(layer, (D_MODEL, D_MODEL)),
        ],
        out_specs=pl.BlockSpec((TM_PROJ, D_MODEL), lambda i: (i, 0)),
        out_shape=jax.ShapeDtypeStruct((t, D_MODEL), F32),
        compiler_params=_compiler_params(("arbitrary",)),
        name="merge",
    )(x2, o_a, o_b, qkvu, qkvu, gates, w_pool, pool_scale, w_a, w_b, w_c, w_out)


def _ffn_kernel(tiles_per_seq, x_ref, g_ref, wu_ref, cw_ref, cb_ref, wd_ref, o_ref,
                halo_ref, h_ref, acc_ref, act_ref):
    first = (pl.program_id(0) % tiles_per_seq) == 0
    groups = TM_FFN // SUBLANES
    x = jnp.swapaxes(x_ref[...].reshape(SUBLANES, groups, D_MODEL), 0, 1).reshape(TM_FFN, D_MODEL)
    h_ref[...] = _rms_norm_rows(x, g_ref[...]).astype(BF16)
    acc_ref[...] = x
    row = lax.broadcasted_iota(jnp.int32, (SUBLANES, 1), 0)

    def up(n0):
        return tuple(_dot(h_ref[...], wu_ref[:, c0:c0 + FF_CHUNK]) for c0 in (n0, D_FF + n0))

    def conv(u, c0):
        cols = slice(c0, c0 + FF_CHUNK)
        prev = jnp.where(first, 0.0, halo_ref[:, cols])
        halo_ref[:, cols] = u[TM_FFN - 2 * SUBLANES:]
        back2 = prev[SUBLANES - 1:SUBLANES]
        back1 = prev[2 * SUBLANES - 1:2 * SUBLANES]
        top1 = jnp.where(row == 0, back1, pltpu.roll(u[TM_FFN - SUBLANES:], 1, 0))
        top2 = jnp.where(row == 0, back2, pltpu.roll(u[TM_FFN - 2 * SUBLANES:TM_FFN - SUBLANES], 1, 0))
        u1 = jnp.concatenate([top1, u[:TM_FFN - SUBLANES]], axis=0)
        u2 = jnp.concatenate([top2, top1, u[:TM_FFN - 2 * SUBLANES]], axis=0)
        return (cb_ref[:, cols] + cw_ref[0:1, cols] * u2 + cw_ref[1:2, cols] * u1
                + cw_ref[2:3, cols] * u)

    chunks = list(range(0, D_FF, FF_CHUNK))
    u_next = up(chunks[0])
    for k, n0 in enumerate(chunks):
        u_gate, u_val = u_next
        if k + 1 < len(chunks):
            u_next = up(chunks[k + 1])
        act = (jax.nn.silu(conv(u_gate, n0)) * conv(u_val, D_FF + n0)).astype(BF16)
        act_ref[:, n0:n0 + FF_CHUNK] = act
    out = acc_ref[...] + _dot(act_ref[...], wd_ref[...])
    o_ref[...] = jnp.swapaxes(out.reshape(groups, SUBLANES, D_MODEL), 0, 1).reshape(TM_FFN, D_MODEL)


def _ffn(x2, gain, w_up, conv_w, conv_b, w_down, seq, layer):
    t = x2.shape[0]
    return pl.pallas_call(
        functools.partial(_ffn_kernel, seq // TM_FFN),
        grid=(t // TM_FFN,),
        in_specs=[
            pl.BlockSpec((TM_FFN, D_MODEL), lambda i: (i, 0)),
            _resident((1, D_MODEL)),
            _layer_resident(layer, (D_MODEL, 2 * D_FF)),
            _resident((3, 2 * D_FF)),
            _resident((1, 2 * D_FF)),
            _layer_resident(layer, (D_FF, D_MODEL)),
        ],
        out_specs=pl.BlockSpec((TM_FFN, D_MODEL), lambda i: (i, 0)),
        out_shape=jax.ShapeDtypeStruct((t, D_MODEL), F32),
        scratch_shapes=[pltpu.VMEM((2 * SUBLANES, 2 * D_FF), F32), pltpu.VMEM((TM_FFN, D_MODEL), BF16),
                        pltpu.VMEM((TM_FFN, D_MODEL), F32), pltpu.VMEM((TM_FFN, D_FF), BF16)],
        compiler_params=_compiler_params(("arbitrary",)),
        name="ffn",
    )(x2, gain, w_up, conv_w, conv_b, w_down)


def kernel(x, norm_mix, w_in, b_gate, q_norm_a, k_norm_a, rel_bias_a, w_pool, pool_scale,
           w_branch_a, w_branch_b, w_branch_c, w_out, norm_ffn, w_up, conv_w, conv_b, w_down):
    b, s, d = x.shape
    assert d == D_MODEL and s % TQ_A == 0 and s % TM_PROJ == 0 and s % (TILES_B * TQ_B) == 0
    depth = w_in.shape[0]
    w_in, w_pool, w_branch_a, w_branch_b, w_branch_c, w_out, w_up, w_down = (
        w.astype(BF16) for w in (w_in, w_pool, w_branch_a, w_branch_b, w_branch_c, w_out, w_up, w_down))
    x2 = x.reshape(b * s, d)
    for l in range(depth):
        gq = jnp.tile(q_norm_a[l], 2)[None, :]
        gk = jnp.tile(k_norm_a[l], 2)[None, :]
        qkvu, gates = _in_proj(x2, norm_mix[l][None, :], w_in, b_gate[l][None, :], gq, gk, l)
        qkvu3 = qkvu.reshape(b, s, QKVU_COLS)
        o_a = _mixer_a(qkvu3, _rel_bias_units(rel_bias_a[l]))
        o_b = _mixer_b(qkvu3)
        x2 = _merge(x2, o_a.reshape(b * s, WIDTH), o_b.reshape(b * s, WIDTH), qkvu, gates,
                    w_pool, pool_scale[l][None, :], w_branch_a, w_branch_b, w_branch_c, w_out, s, l)
        x2 = _ffn(x2, norm_ffn[l][None, :], w_up, conv_w[l], conv_b[l][None, :], w_down, s, l)
    return x2.reshape(b, s, d)
```

```python
import functools
import math

import jax
import jax.numpy as jnp
import numpy as np
from jax import lax
from jax.experimental import pallas as pl
from jax.experimental.pallas import tpu as pltpu

D_MODEL = 1024
CHUNK = 64
N_LEFT_CHUNKS = 8
HEAD_DIM = 64
N_HEADS = 8
N_PAIRS = N_HEADS // 2
WIDTH = N_HEADS * HEAD_DIM
POOL_WINDOWS = (2, 4, 8, 16)
POOL_GROUP_DIM = 128
MAX_REL_DIST = 2 * CHUNK
D_FF = 2816
EPS = 1e-6
MASK_VALUE = -1e30

QKVU_COLS = 7 * WIDTH
GATE_COLS = 3 * D_MODEL

LANES = 128
SUBLANES = 8
VMEM_LIMIT_BYTES = 56 * 1024 * 1024

TM_PROJ = 1024
TN_PROJ = 512
TM_FFN = 1024
LOOKBACK_A = N_LEFT_CHUNKS * CHUNK
TQ_A = 2048
UNIT_A = 2 * CHUNK
BAND_A = (N_LEFT_CHUNKS + 2) * CHUNK
GROUP_A = 4
START_UNITS_A = LOOKBACK_A // UNIT_A
TQ_B = 256
TILES_B = 8
GROUP_B = 8
KB_B = 256
FF_CHUNK = 256
HALO_POOL = 16

LOG2E = 1.4426950408889634
CARRY_FLOOR_B = -152.0

BF16 = jnp.bfloat16
F32 = jnp.float32


def _dot(a, b):
    return jnp.dot(a, b, preferred_element_type=F32)


def _dot_nt(a, b):
    return lax.dot_general(a, b, (((1,), (1,)), ((), ())), preferred_element_type=F32)


def _rms_norm_rows(x, gain):
    ms = jnp.mean(x * x, axis=-1, keepdims=True)
    return x * lax.rsqrt(ms + EPS) * gain


def _compiler_params(semantics):
    return pltpu.CompilerParams(dimension_semantics=semantics,
                                vmem_limit_bytes=VMEM_LIMIT_BYTES)


def _resident(shape):
    zeros = (0,) * len(shape)
    return pl.BlockSpec(shape, lambda *_: zeros, pipeline_mode=pl.Buffered(1))


def _layer_resident(layer, shape):
    index = (layer,) + (0,) * len(shape)
    return pl.BlockSpec((None,) + tuple(shape), lambda *_: index, pipeline_mode=pl.Buffered(1))


def _head_pair_norm(x, gain, low_lanes):
    sq = x * x
    s_low = jnp.sum(jnp.where(low_lanes, sq, 0.0), axis=-1, keepdims=True)
    s_high = jnp.sum(jnp.where(low_lanes, 0.0, sq), axis=-1, keepdims=True)
    ms = jnp.where(low_lanes, s_low, s_high) * (1.0 / HEAD_DIM)
    return x * lax.rsqrt(ms + EPS) * gain


def _low_lanes():
    return lax.broadcasted_iota(jnp.int32, (1, LANES), 1) < HEAD_DIM


def _in_proj_kernel(x_ref, g_ref, w_ref, b_ref, gq_ref, gk_ref, qkvu_ref, gate_ref):
    h = _rms_norm_rows(x_ref[...], g_ref[...]).astype(BF16)
    low_lanes = _low_lanes()
    score_scale = 1.0 / math.sqrt(HEAD_DIM)
    head_norm = {0: gq_ref, WIDTH: gk_ref}
    scaled = {0: score_scale, 3 * WIDTH: score_scale}
    for n0 in range(0, QKVU_COLS, TN_PROJ):
        y = _dot(h, w_ref[:, n0:n0 + TN_PROJ])
        if n0 in head_norm:
            y = jnp.concatenate(
                [_head_pair_norm(y[:, c:c + LANES], head_norm[n0][...], low_lanes)
                 for c in range(0, TN_PROJ, LANES)], axis=1)
        if n0 in scaled:
            y = y * scaled[n0]
        qkvu_ref[:, n0:n0 + TN_PROJ] = y.astype(BF16)
    for n0 in range(0, GATE_COLS, TN_PROJ):
        z = _dot(h, w_ref[:, QKVU_COLS + n0:QKVU_COLS + n0 + TN_PROJ]) + b_ref[:, n0:n0 + TN_PROJ]
        gate_ref[:, n0:n0 + TN_PROJ] = jax.nn.sigmoid(z).astype(BF16)


def _in_proj(x2, gain, w_in, b_gate, gq, gk, layer):
    t = x2.shape[0]
    return pl.pallas_call(
        _in_proj_kernel,
        grid=(t // TM_PROJ,),
        in_specs=[
            pl.BlockSpec((TM_PROJ, D_MODEL), lambda i: (i, 0)),
            _resident((1, D_MODEL)),
            _layer_resident(layer, (D_MODEL, QKVU_COLS + GATE_COLS)),
            _resident((1, GATE_COLS)),
            _resident((1, LANES)),
            _resident((1, LANES)),
        ],
        out_specs=[
            pl.BlockSpec((TM_PROJ, QKVU_COLS), lambda i: (i, 0)),
            pl.BlockSpec((TM_PROJ, GATE_COLS), lambda i: (i, 0)),
        ],
        out_shape=[
            jax.ShapeDtypeStruct((t, QKVU_COLS), BF16),
            jax.ShapeDtypeStruct((t, GATE_COLS), BF16),
        ],
        compiler_params=_compiler_params(("arbitrary",)),
        name="in_proj",
    )(x2, gain, w_in, b_gate, gq, gk)


def _mixer_a_kernel(q_ref, k_ref, v_ref, bias_ref, o_ref, q_scr, kt_scr, v_scr):
    t = pl.program_id(2)
    low_lanes = _low_lanes()
    n_units = TQ_A // UNIT_A

    zero = jnp.zeros((), BF16)
    for u in range(n_units):
        qu = q_ref[0, u * UNIT_A:(u + 1) * UNIT_A, :]
        q_scr[u, 0:UNIT_A, :] = jnp.where(low_lanes, qu, zero)
        q_scr[u, UNIT_A:2 * UNIT_A, :] = jnp.where(low_lanes, zero, qu)

    @pl.when(t == 0)
    def _():
        kt_scr[:, 0:LOOKBACK_A] = jnp.zeros((LANES, LOOKBACK_A), BF16)
        v_scr[0:LOOKBACK_A, :] = jnp.zeros((LOOKBACK_A, LANES), BF16)

    @pl.when(t > 0)
    def _():
        kt_scr[:, 0:LOOKBACK_A] = kt_scr[:, TQ_A:TQ_A + LOOKBACK_A]
        v_scr[0:LOOKBACK_A, :] = v_scr[TQ_A:TQ_A + LOOKBACK_A, :]

    kt_scr[:, LOOKBACK_A:LOOKBACK_A + TQ_A] = k_ref[0].astype(F32).T.astype(BF16)
    v_scr[LOOKBACK_A:LOOKBACK_A + TQ_A, :] = v_ref[0]

    for u0 in range(0, n_units, GROUP_A):
        group = range(u0, min(u0 + GROUP_A, n_units))
        probs, denoms = {}, {}
        for u in group:
            k0 = u * UNIT_A
            variant = jnp.where(t == 0, u + 1, 0) if u < START_UNITS_A else 0
            bias = jnp.concatenate([bias_ref[0, variant], bias_ref[1, variant]], axis=0)
            logits = _dot(q_scr[u], kt_scr[:, k0:k0 + BAND_A]) + bias
            m = jnp.max(logits, axis=-1, keepdims=True)
            p = jnp.exp(logits - m)
            denoms[u] = jnp.sum(p, axis=-1, keepdims=True)
            probs[u] = p.astype(BF16)
        for u in group:
            k0 = u * UNIT_A
            pv = _dot(probs[u], v_scr[k0:k0 + BAND_A, :]) / denoms[u]
            o_ref[0, k0:k0 + UNIT_A, :] = jnp.where(low_lanes, pv[0:UNIT_A], pv[UNIT_A:]).astype(BF16)


def _mixer_a(qkvu3, bias, layer):
    b, s, _ = qkvu3.shape
    blk = (1, TQ_A, LANES)
    return pl.pallas_call(
        _mixer_a_kernel,
        grid=(b, N_PAIRS, s // TQ_A),
        in_specs=[
            pl.BlockSpec(blk, lambda bi, p, t: (bi, t, p)),
            pl.BlockSpec(blk, lambda bi, p, t: (bi, t, N_PAIRS + p)),
            pl.BlockSpec(blk, lambda bi, p, t: (bi, t, 2 * N_PAIRS + p)),
            pl.BlockSpec((None, 2, 1 + START_UNITS_A, UNIT_A, BAND_A), lambda bi, p, t: (layer, p, 0, 0, 0)),
        ],
        out_specs=pl.BlockSpec(blk, lambda bi, p, t: (bi, t, p)),
        out_shape=jax.ShapeDtypeStruct((b, s, WIDTH), BF16),
        scratch_shapes=[
            pltpu.VMEM((TQ_A // UNIT_A, 2 * UNIT_A, LANES), BF16),
            pltpu.VMEM((LANES, LOOKBACK_A + TQ_A), BF16),
            pltpu.VMEM((LOOKBACK_A + TQ_A, LANES), BF16),
        ],
        compiler_params=_compiler_params(("arbitrary", "arbitrary", "arbitrary")),
        name="mixer_a",
    )(qkvu3, qkvu3, qkvu3, bias)


def _rel_bias_units(rel_bias):
    period = UNIT_A + BAND_A
    d = np.arange(period)
    d = np.where(d < BAND_A, d, d - period)
    idx = np.clip(N_LEFT_CHUNKS * CHUNK - d, -(CHUNK - 1), MAX_REL_DIST) + (CHUNK - 1)
    vec = rel_bias.astype(F32)[..., idx]
    lead = rel_bias.shape[:-1]
    rows = jnp.tile(vec, (1, 1, UNIT_A))[..., :UNIT_A * (period - 1)]
    toeplitz = rows.reshape(*lead, UNIT_A, period - 1)[..., :BAND_A]
    q_chunk = np.arange(UNIT_A)[:, None] // CHUNK
    k_chunk = np.arange(BAND_A)[None, :] // CHUNK
    in_band = (k_chunk >= q_chunk) & (k_chunk <= q_chunk + N_LEFT_CHUNKS)
    key = np.arange(BAND_A)[None, :]
    keep = np.stack([in_band] + [in_band & (key + u * UNIT_A >= LOOKBACK_A) for u in range(START_UNITS_A)])
    return jnp.where(keep, toeplitz[..., None, :, :], MASK_VALUE)


def _mixer_b_kernel(q_ref, k_ref, v_ref, cum_ref, o_ref, q_scr, acc_ref, carry_ref, top_ref):
    m = pl.program_id(2)
    low_lanes = _low_lanes()
    zero = jnp.zeros((), BF16)
    for tile in range(TILES_B):
        q = q_ref[0, tile * TQ_B:(tile + 1) * TQ_B, :]
        q_scr[tile, 0:TQ_B, :] = jnp.where(low_lanes, q, zero)
        q_scr[tile, TQ_B:2 * TQ_B, :] = jnp.where(low_lanes, zero, q)

    def block_start(j):
        return pl.multiple_of(j * KB_B, KB_B)

    def scores(tile, j, diagonal):
        neg_z = _dot_nt(q_scr[tile], k_ref[0, pl.ds(block_start(j), KB_B), :]) * (-LOG2E)
        soft = jnp.log2(1.0 + jnp.exp2(-jnp.abs(neg_z)))
        log_keep = jnp.minimum(neg_z, 0.0) - soft
        log_beta = log_keep - neg_z
        before = None
        if diagonal:
            row = lax.broadcasted_iota(jnp.int32, (2 * TQ_B, KB_B), 0)
            col = lax.broadcasted_iota(jnp.int32, (2 * TQ_B, KB_B), 1)
            before = col < jnp.where(row >= TQ_B, row - TQ_B, row)
            log_keep = jnp.where(before, log_keep, 0.0)
        return log_keep.astype(BF16), log_beta, log_keep[:, 0:1], before

    def weights(scored, carry):
        keep, log_beta, first_keep, before = scored
        tail = _dot(keep, cum_ref[...])
        exponent = log_beta + tail
        if carry is not None:
            exponent = exponent + jnp.concatenate([carry] * (KB_B // LANES), axis=1)
        w = jnp.exp2(exponent)
        if before is not None:
            w = jnp.where(before, w, 0.0)
        block_total = jnp.broadcast_to(tail[:, 0:1] + first_keep, (2 * TQ_B, LANES))
        return w.astype(BF16), (block_total if carry is None else carry + block_total)

    def weighted_values(w, j):
        return _dot(w, v_ref[0, pl.ds(block_start(j), KB_B), :])

    def first_visits(tiles, tiles_with_previous):
        diag_block = {t: TILES_B * m + t for t in tiles}
        scored_diag = {t: scores(t, diag_block[t], True) for t in tiles}
        scored_prev = {t: scores(t, diag_block[t] - 1, False) for t in tiles_with_previous}
        w_diag, w_prev, carry = {}, {}, {}
        for t in tiles:
            w_diag[t], carry[t] = weights(scored_diag[t], None)
        for t in tiles_with_previous:
            w_prev[t], carry[t] = weights(scored_prev[t], carry[t])
        for t in tiles:
            pv = weighted_values(w_diag[t], diag_block[t])
            if t in tiles_with_previous:
                pv = pv + weighted_values(w_prev[t], diag_block[t] - 1)
            acc_ref[t] = pv
            carry_ref[t] = carry[t]
            top_ref[t] = jnp.max(carry[t])

    def first_visits_grouped(first_tile_has_previous):
        for t0 in range(0, TILES_B, GROUP_B):
            tiles = list(range(t0, min(t0 + GROUP_B, TILES_B)))
            first_visits(tiles, [t for t in tiles if t > 0 or first_tile_has_previous])

    @pl.when(m == 0)
    def _():
        first_visits_grouped(False)

    @pl.when(m > 0)
    def _():
        first_visits_grouped(True)

    def is_live(top):
        return (top >= CARRY_FLOOR_B).astype(jnp.int32)

    any_live = jnp.int32(0)
    for t in range(TILES_B):
        has_more_blocks = (TILES_B * m + t >= 2).astype(jnp.int32)
        any_live = jnp.maximum(any_live, has_more_blocks * is_live(top_ref[t]))

    @pl.when(any_live > 0)
    def _():
        for t in range(TILES_B):
            diag_block = TILES_B * m + t

            def more_keys(state, diag_block=diag_block):
                step, live = state
                return jnp.logical_and(step < diag_block, live > 0)

            def body(state, t=t, diag_block=diag_block):
                step, _ = state
                j = diag_block - 1 - step
                w, carry = weights(scores(t, j, False), carry_ref[t])
                acc_ref[t] += weighted_values(w, j)
                carry_ref[t] = carry
                return step + 1, is_live(jnp.max(carry))

            lax.while_loop(more_keys, body, (jnp.int32(1), is_live(top_ref[t])))

    for t in range(TILES_B):
        o_ref[0, t * TQ_B:(t + 1) * TQ_B, :] = jnp.where(
            low_lanes, acc_ref[t, 0:TQ_B, :], acc_ref[t, TQ_B:2 * TQ_B, :]).astype(BF16)


def _cumulative_matrix():
    j = np.arange(KB_B)[:, None]
    s = np.arange(KB_B)[None, :]
    return jnp.asarray((j > s).astype(np.float32), BF16)


def _mixer_b(qkvu3):
    b, s, _ = qkvu3.shape
    return pl.pallas_call(
        _mixer_b_kernel,
        grid=(b, N_PAIRS, s // (TILES_B * TQ_B)),
        in_specs=[
            pl.BlockSpec((1, TILES_B * TQ_B, LANES), lambda bi, p, i: (bi, i, 3 * N_PAIRS + p)),
            pl.BlockSpec((1, s, LANES), lambda bi, p, i: (bi, 0, 4 * N_PAIRS + p)),
            pl.BlockSpec((1, s, LANES), lambda bi, p, i: (bi, 0, 5 * N_PAIRS + p)),
            _resident((KB_B, KB_B)),
        ],
        out_specs=pl.BlockSpec((1, TILES_B * TQ_B, LANES), lambda bi, p, i: (bi, i, p)),
        out_shape=jax.ShapeDtypeStruct((b, s, WIDTH), BF16),
        scratch_shapes=[
            pltpu.VMEM((TILES_B, 2 * TQ_B, LANES), BF16),
            pltpu.VMEM((TILES_B, 2 * TQ_B, LANES), F32),
            pltpu.VMEM((TILES_B, 2 * TQ_B, LANES), F32),
            pltpu.SMEM((TILES_B,), F32),
        ],
        compiler_params=_compiler_params(("arbitrary", "arbitrary", "arbitrary")),
        name="mixer_b",
    )(qkvu3, qkvu3, qkvu3, _cumulative_matrix())


def _merge_kernel(tiles_per_seq, x_ref, oa_ref, ob_ref, u_ref, uh_ref, gate_ref,
                  wp_ref, ps_ref, wa_ref, wb_ref, wc_ref, wo_ref, o_ref):
    first = (pl.program_id(0) % tiles_per_seq) == 0
    u = u_ref[...].astype(F32)
    halo = jnp.where(first, 0.0, uh_ref[...].astype(F32))
    ext = jnp.concatenate([halo, u], axis=0)
    pos = lax.broadcasted_iota(jnp.int32, (TM_PROJ, 1), 0)

    pooled_groups = []
    for g, win in enumerate(POOL_WINDOWS):
        cols = slice(g * POOL_GROUP_DIM, (g + 1) * POOL_GROUP_DIM)
        a = ext[:, cols]
        step = 1
        while step < win:
            a = a + pltpu.roll(a, step, 0)
            step *= 2
        win_sum = a[HALO_POOL:]
        count = jnp.where(first, jnp.minimum(pos + 1, win), win).astype(F32)
        pooled = win_sum / count - u[:, cols]
        mixed = _dot(pooled.astype(BF16), wp_ref[g])
        pooled_groups.append(mixed * ps_ref[:, cols])
    o_c = jnp.concatenate(pooled_groups, axis=1).astype(BF16)

    merged = (gate_ref[:, 0:D_MODEL].astype(F32) * _dot(oa_ref[...], wa_ref[...])
              + gate_ref[:, D_MODEL:2 * D_MODEL].astype(F32) * _dot(ob_ref[...], wb_ref[...])
              + gate_ref[:, 2 * D_MODEL:3 * D_MODEL].astype(F32) * _dot(o_c, wc_ref[...]))
    o_ref[...] = x_ref[...] + _dot(merged.astype(BF16), wo_ref[...])


def _merge(x2, o_a, o_b, qkvu, gates, w_pool, pool_scale, w_a, w_b, w_c, w_out, seq, layer):
    t = x2.shape[0]
    u_block = QKVU_COLS // WIDTH - 1
    halo_blocks = TM_PROJ // HALO_POOL
    return pl.pallas_call(
        functools.partial(_merge_kernel, seq // TM_PROJ),
        grid=(t // TM_PROJ,),
        in_specs=[
            pl.BlockSpec((TM_PROJ, D_MODEL), lambda i: (i, 0)),
            pl.BlockSpec((TM_PROJ, WIDTH), lambda i: (i, 0)),
            pl.BlockSpec((TM_PROJ, WIDTH), lambda i: (i, 0)),
            pl.BlockSpec((TM_PROJ, WIDTH), lambda i: (i, u_block)),
            pl.BlockSpec((HALO_POOL, WIDTH), lambda i: (jnp.maximum(i * halo_blocks - 1, 0), u_block)),
            pl.BlockSpec((TM_PROJ, GATE_COLS), lambda i: (i, 0)),
            _layer_resident(layer, (len(POOL_WINDOWS), POOL_GROUP_DIM, POOL_GROUP_DIM)),
            _resident((1, WIDTH)),
            _layer_resident(layer, (WIDTH, D_MODEL)),
            _layer_resident(layer, (WIDTH, D_MODEL)),
            _layer_resident(layer, (WIDTH, D_MODEL)),
            _layer_resident(layer, (D_MODEL, D_MODEL)),
        ],
        out_specs=pl.BlockSpec((TM_PROJ, D_MODEL), lambda i: (i, 0)),
        out_shape=jax.ShapeDtypeStruct((t, D_MODEL), F32),
        compiler_params=_compiler_params(("arbitrary",)),
        name="merge",
    )(x2, o_a, o_b, qkvu, qkvu, gates, w_pool, pool_scale, w_a, w_b, w_c, w_out)


def _ffn_kernel(tiles_per_seq, x_ref, g_ref, wu_ref, cw_ref, cb_ref, wd_ref, o_ref,
                halo_ref, h_ref, acc_ref, act_ref):
    first = (pl.program_id(0) % tiles_per_seq) == 0
    groups = TM_FFN // SUBLANES
    x = jnp.swapaxes(x_ref[...].reshape(SUBLANES, groups, D_MODEL), 0, 1).reshape(TM_FFN, D_MODEL)
    h_ref[...] = _rms_norm_rows(x, g_ref[...]).astype(BF16)
    acc_ref[...] = x
    row = lax.broadcasted_iota(jnp.int32, (SUBLANES, 1), 0)

    def up(n0):
        return tuple(_dot(h_ref[...], wu_ref[:, c0:c0 + FF_CHUNK]) for c0 in (n0, D_FF + n0))

    def conv(u, c0):
        cols = slice(c0, c0 + FF_CHUNK)
        prev = jnp.where(first, 0.0, halo_ref[:, cols])
        halo_ref[:, cols] = u[TM_FFN - 2 * SUBLANES:]
        back2 = prev[SUBLANES - 1:SUBLANES]
        back1 = prev[2 * SUBLANES - 1:2 * SUBLANES]
        top1 = jnp.where(row == 0, back1, pltpu.roll(u[TM_FFN - SUBLANES:], 1, 0))
        top2 = jnp.where(row == 0, back2, pltpu.roll(u[TM_FFN - 2 * SUBLANES:TM_FFN - SUBLANES], 1, 0))
        u1 = jnp.concatenate([top1, u[:TM_FFN - SUBLANES]], axis=0)
        u2 = jnp.concatenate([top2, top1, u[:TM_FFN - 2 * SUBLANES]], axis=0)
        return (cb_ref[:, cols] + cw_ref[0:1, cols] * u2 + cw_ref[1:2, cols] * u1
                + cw_ref[2:3, cols] * u)

    chunks = list(range(0, D_FF, FF_CHUNK))
    u_next = up(chunks[0])
    for k, n0 in enumerate(chunks):
        u_gate, u_val = u_next
        if k + 1 < len(chunks):
            u_next = up(chunks[k + 1])
        act = (jax.nn.silu(conv(u_gate, n0)) * conv(u_val, D_FF + n0)).astype(BF16)
        act_ref[:, n0:n0 + FF_CHUNK] = act
    out = acc_ref[...] + _dot(act_ref[...], wd_ref[...])
    o_ref[...] = jnp.swapaxes(out.reshape(groups, SUBLANES, D_MODEL), 0, 1).reshape(TM_FFN, D_MODEL)


def _ffn(x2, gain, w_up, conv_w, conv_b, w_down, seq, layer):
    t = x2.shape[0]
    return pl.pallas_call(
        functools.partial(_ffn_kernel, seq // TM_FFN),
        grid=(t // TM_FFN,),
        in_specs=[
            pl.BlockSpec((TM_FFN, D_MODEL), lambda i: (i, 0)),
            _resident((1, D_MODEL)),
            _layer_resident(layer, (D_MODEL, 2 * D_FF)),
            _resident((3, 2 * D_FF)),
            _resident((1, 2 * D_FF)),
            _layer_resident(layer, (D_FF, D_MODEL)),
        ],
        out_specs=pl.BlockSpec((TM_FFN, D_MODEL), lambda i: (i, 0)),
        out_shape=jax.ShapeDtypeStruct((t, D_MODEL), F32),
        scratch_shapes=[pltpu.VMEM((2 * SUBLANES, 2 * D_FF), F32), pltpu.VMEM((TM_FFN, D_MODEL), BF16),
                        pltpu.VMEM((TM_FFN, D_MODEL), F32), pltpu.VMEM((TM_FFN, D_FF), BF16)],
        compiler_params=_compiler_params(("arbitrary",)),
        name="ffn",
    )(x2, gain, w_up, conv_w, conv_b, w_down)


def kernel(x, norm_mix, w_in, b_gate, q_norm_a, k_norm_a, rel_bias_a, w_pool, pool_scale,
           w_branch_a, w_branch_b, w_branch_c, w_out, norm_ffn, w_up, conv_w, conv_b, w_down):
    b, s, d = x.shape
    assert d == D_MODEL and s % TQ_A == 0 and s % TM_PROJ == 0 and s % (TILES_B * TQ_B) == 0
    depth = w_in.shape[0]
    w_in, w_pool, w_branch_a, w_branch_b, w_branch_c, w_out, w_up, w_down = (
        w.astype(BF16) for w in (w_in, w_pool, w_branch_a, w_branch_b, w_branch_c, w_out, w_up, w_down))
    bias_a = _rel_bias_units(rel_bias_a)
    x2 = x.reshape(b * s, d)
    for l in range(depth):
        gq = jnp.tile(q_norm_a[l], 2)[None, :]
        gk = jnp.tile(k_norm_a[l], 2)[None, :]
        qkvu, gates = _in_proj(x2, norm_mix[l][None, :], w_in, b_gate[l][None, :], gq, gk, l)
        qkvu3 = qkvu.reshape(b, s, QKVU_COLS)
        o_a = _mixer_a(qkvu3, bias_a, l)
        o_b = _mixer_b(qkvu3)
        x2 = _merge(x2, o_a.reshape(b * s, WIDTH), o_b.reshape(b * s, WIDTH), qkvu, gates,
                    w_pool, pool_scale[l][None, :], w_branch_a, w_branch_b, w_branch_c, w_out, s, l)
        x2 = _ffn(x2, norm_ffn[l][None, :], w_up, conv_w[l], conv_b[l][None, :], w_down, s, l)
    return x2.reshape(b, s, d)
```

```python
import functools
import math

import jax
import jax.numpy as jnp
import numpy as np
from jax import lax
from jax.experimental import pallas as pl
from jax.experimental.pallas import tpu as pltpu

D_MODEL = 1024
CHUNK = 64
N_LEFT_CHUNKS = 8
HEAD_DIM = 64
N_HEADS = 8
N_PAIRS = N_HEADS // 2
WIDTH = N_HEADS * HEAD_DIM
POOL_WINDOWS = (2, 4, 8, 16)
POOL_GROUP_DIM = 128
MAX_REL_DIST = 2 * CHUNK
D_FF = 2816
EPS = 1e-6
MASK_VALUE = -1e30

QKVU_COLS = 7 * WIDTH
GATE_COLS = 3 * D_MODEL

LANES = 128
SUBLANES = 8
VMEM_LIMIT_BYTES = 56 * 1024 * 1024

TM_PROJ = 1024
TN_PROJ = 512
TM_FFN = 1024
LOOKBACK_A = N_LEFT_CHUNKS * CHUNK
TQ_A = 2048
UNIT_A = 2 * CHUNK
BAND_A = (N_LEFT_CHUNKS + 2) * CHUNK
GROUP_A = 4
START_UNITS_A = LOOKBACK_A // UNIT_A
TQ_B = 256
TILES_B = 8
GROUP_B = 8
KB_B = 256
FF_CHUNK = 256
HALO_POOL = 16

LOG2E = 1.4426950408889634
CARRY_FLOOR_B = -152.0

BF16 = jnp.bfloat16
F32 = jnp.float32


def _dot(a, b):
    return jnp.dot(a, b, preferred_element_type=F32)


def _dot_nt(a, b):
    return lax.dot_general(a, b, (((1,), (1,)), ((), ())), preferred_element_type=F32)


def _rms_norm_rows(x, gain):
    ms = jnp.mean(x * x, axis=-1, keepdims=True)
    return x * lax.rsqrt(ms + EPS) * gain


def _compiler_params(semantics):
    return pltpu.CompilerParams(dimension_semantics=semantics,
                                vmem_limit_bytes=VMEM_LIMIT_BYTES)


def _resident(shape):
    zeros = (0,) * len(shape)
    return pl.BlockSpec(shape, lambda *_: zeros, pipeline_mode=pl.Buffered(1))


def _layer_resident(layer, shape):
    index = (layer,) + (0,) * len(shape)
    return pl.BlockSpec((None,) + tuple(shape), lambda *_: index, pipeline_mode=pl.Buffered(1))


def _head_pair_norm(x, gain, low_lanes):
    sq = x * x
    s_low = jnp.sum(jnp.where(low_lanes, sq, 0.0), axis=-1, keepdims=True)
    s_high = jnp.sum(jnp.where(low_lanes, 0.0, sq), axis=-1, keepdims=True)
    ms = jnp.where(low_lanes, s_low, s_high) * (1.0 / HEAD_DIM)
    return x * lax.rsqrt(ms + EPS) * gain


def _low_lanes():
    return lax.broadcasted_iota(jnp.int32, (1, LANES), 1) < HEAD_DIM


def _in_proj_kernel(x_ref, g_ref, w_ref, b_ref, gq_ref, gk_ref, qkvu_ref, gate_ref):
    h = _rms_norm_rows(x_ref[...], g_ref[...]).astype(BF16)
    low_lanes = _low_lanes()
    score_scale = 1.0 / math.sqrt(HEAD_DIM)
    head_norm = {0: gq_ref, WIDTH: gk_ref}
    scaled = {0: score_scale, 3 * WIDTH: score_scale}
    for n0 in range(0, QKVU_COLS, TN_PROJ):
        y = _dot(h, w_ref[:, n0:n0 + TN_PROJ])
        if n0 in head_norm:
            y = jnp.concatenate(
                [_head_pair_norm(y[:, c:c + LANES], head_norm[n0][...], low_lanes)
                 for c in range(0, TN_PROJ, LANES)], axis=1)
        if n0 in scaled:
            y = y * scaled[n0]
        qkvu_ref[:, n0:n0 + TN_PROJ] = y.astype(BF16)
    for n0 in range(0, GATE_COLS, TN_PROJ):
        z = _dot(h, w_ref[:, QKVU_COLS + n0:QKVU_COLS + n0 + TN_PROJ]) + b_ref[:, n0:n0 + TN_PROJ]
        gate_ref[:, n0:n0 + TN_PROJ] = (0.5 * jnp.tanh(0.5 * z) + 0.5).astype(BF16)


def _in_proj(x2, gain, w_in, b_gate, gq, gk, layer):
    t = x2.shape[0]
    return pl.pallas_call(
        _in_proj_kernel,
        grid=(t // TM_PROJ,),
        in_specs=[
            pl.BlockSpec((TM_PROJ, D_MODEL), lambda i: (i, 0)),
            _resident((1, D_MODEL)),
            _layer_resident(layer, (D_MODEL, QKVU_COLS + GATE_COLS)),
            _resident((1, GATE_COLS)),
            _resident((1, LANES)),
            _resident((1, LANES)),
        ],
        out_specs=[
            pl.BlockSpec((TM_PROJ, QKVU_COLS), lambda i: (i, 0)),
            pl.BlockSpec((TM_PROJ, GATE_COLS), lambda i: (i, 0)),
        ],
        out_shape=[
            jax.ShapeDtypeStruct((t, QKVU_COLS), BF16),
            jax.ShapeDtypeStruct((t, GATE_COLS), BF16),
        ],
        compiler_params=_compiler_params(("arbitrary",)),
        name="in_proj",
    )(x2, gain, w_in, b_gate, gq, gk)


def _mixer_a_kernel(q_ref, k_ref, v_ref, bias_ref, o_ref, q_scr, kt_scr, v_scr):
    t = pl.program_id(2)
    low_lanes = _low_lanes()
    n_units = TQ_A // UNIT_A

    zero = jnp.zeros((), BF16)
    for u in range(n_units):
        qu = q_ref[0, u * UNIT_A:(u + 1) * UNIT_A, :]
        q_scr[u, 0:UNIT_A, :] = jnp.where(low_lanes, qu, zero)
        q_scr[u, UNIT_A:2 * UNIT_A, :] = jnp.where(low_lanes, zero, qu)

    @pl.when(t == 0)
    def _():
        kt_scr[:, 0:LOOKBACK_A] = jnp.zeros((LANES, LOOKBACK_A), BF16)
        v_scr[0:LOOKBACK_A, :] = jnp.zeros((LOOKBACK_A, LANES), BF16)

    @pl.when(t > 0)
    def _():
        kt_scr[:, 0:LOOKBACK_A] = kt_scr[:, TQ_A:TQ_A + LOOKBACK_A]
        v_scr[0:LOOKBACK_A, :] = v_scr[TQ_A:TQ_A + LOOKBACK_A, :]

    kt_scr[:, LOOKBACK_A:LOOKBACK_A + TQ_A] = k_ref[0].astype(F32).T.astype(BF16)
    v_scr[LOOKBACK_A:LOOKBACK_A + TQ_A, :] = v_ref[0]

    for u0 in range(0, n_units, GROUP_A):
        group = range(u0, min(u0 + GROUP_A, n_units))
        probs, denoms = {}, {}
        for u in group:
            k0 = u * UNIT_A
            variant = jnp.where(t == 0, u + 1, 0) if u < START_UNITS_A else 0
            bias = jnp.concatenate([bias_ref[0, variant], bias_ref[1, variant]], axis=0)
            logits = _dot(q_scr[u], kt_scr[:, k0:k0 + BAND_A]) + bias
            m = jnp.max(logits, axis=-1, keepdims=True)
            p = jnp.exp(logits - m)
            denoms[u] = jnp.sum(p, axis=-1, keepdims=True)
            probs[u] = p.astype(BF16)
        for u in group:
            k0 = u * UNIT_A
            pv = _dot(probs[u], v_scr[k0:k0 + BAND_A, :]) / denoms[u]
            o_ref[0, k0:k0 + UNIT_A, :] = jnp.where(low_lanes, pv[0:UNIT_A], pv[UNIT_A:]).astype(BF16)


def _mixer_a(qkvu3, bias, layer):
    b, s, _ = qkvu3.shape
    blk = (1, TQ_A, LANES)
    return pl.pallas_call(
        _mixer_a_kernel,
        grid=(b, N_PAIRS, s // TQ_A),
        in_specs=[
            pl.BlockSpec(blk, lambda bi, p, t: (bi, t, p)),
            pl.BlockSpec(blk, lambda bi, p, t: (bi, t, N_PAIRS + p)),
            pl.BlockSpec(blk, lambda bi, p, t: (bi, t, 2 * N_PAIRS + p)),
            pl.BlockSpec((None, 2, 1 + START_UNITS_A, UNIT_A, BAND_A), lambda bi, p, t: (layer, p, 0, 0, 0)),
        ],
        out_specs=pl.BlockSpec(blk, lambda bi, p, t: (bi, t, p)),
        out_shape=jax.ShapeDtypeStruct((b, s, WIDTH), BF16),
        scratch_shapes=[
            pltpu.VMEM((TQ_A // UNIT_A, 2 * UNIT_A, LANES), BF16),
            pltpu.VMEM((LANES, LOOKBACK_A + TQ_A), BF16),
            pltpu.VMEM((LOOKBACK_A + TQ_A, LANES), BF16),
        ],
        compiler_params=_compiler_params(("arbitrary", "arbitrary", "arbitrary")),
        name="mixer_a",
    )(qkvu3, qkvu3, qkvu3, bias)


def _rel_bias_units(rel_bias):
    period = UNIT_A + BAND_A
    d = np.arange(period)
    d = np.where(d < BAND_A, d, d - period)
    idx = np.clip(N_LEFT_CHUNKS * CHUNK - d, -(CHUNK - 1), MAX_REL_DIST) + (CHUNK - 1)
    vec = rel_bias.astype(F32)[..., idx]
    lead = rel_bias.shape[:-1]
    rows = jnp.tile(vec, (1, 1, UNIT_A))[..., :UNIT_A * (period - 1)]
    toeplitz = rows.reshape(*lead, UNIT_A, period - 1)[..., :BAND_A]
    q_chunk = np.arange(UNIT_A)[:, None] // CHUNK
    k_chunk = np.arange(BAND_A)[None, :] // CHUNK
    in_band = (k_chunk >= q_chunk) & (k_chunk <= q_chunk + N_LEFT_CHUNKS)
    key = np.arange(BAND_A)[None, :]
    keep = np.stack([in_band] + [in_band & (key + u * UNIT_A >= LOOKBACK_A) for u in range(START_UNITS_A)])
    return jnp.where(keep, toeplitz[..., None, :, :], MASK_VALUE)


def _mixer_b_kernel(q_ref, k_ref, v_ref, cum_ref, o_ref, q_scr, acc_ref, carry_ref, top_ref):
    m = pl.program_id(2)
    low_lanes = _low_lanes()
    zero = jnp.zeros((), BF16)
    for tile in range(TILES_B):
        q = q_ref[0, tile * TQ_B:(tile + 1) * TQ_B, :]
        q_scr[tile, 0:TQ_B, :] = jnp.where(low_lanes, q, zero)
        q_scr[tile, TQ_B:2 * TQ_B, :] = jnp.where(low_lanes, zero, q)

    def block_start(j):
        return pl.multiple_of(j * KB_B, KB_B)

    def scores(tile, j, diagonal):
        neg_z = _dot_nt(q_scr[tile], k_ref[0, pl.ds(block_start(j), KB_B), :]) * (-LOG2E)
        soft = jnp.log2(1.0 + jnp.exp2(-jnp.abs(neg_z)))
        log_keep = jnp.minimum(neg_z, 0.0) - soft
        log_beta = log_keep - neg_z
        before = None
        if diagonal:
            row = lax.broadcasted_iota(jnp.int32, (2 * TQ_B, KB_B), 0)
            col = lax.broadcasted_iota(jnp.int32, (2 * TQ_B, KB_B), 1)
            before = col < jnp.where(row >= TQ_B, row - TQ_B, row)
            log_keep = jnp.where(before, log_keep, 0.0)
        return log_keep.astype(BF16), log_beta, log_keep[:, 0:1], before

    def weights(scored, carry):
        keep, log_beta, first_keep, before = scored
        tail = _dot(keep, cum_ref[...])
        exponent = log_beta + tail
        if carry is not None:
            exponent = exponent + jnp.concatenate([carry] * (KB_B // LANES), axis=1)
        w = jnp.exp2(exponent)
        if before is not None:
            w = jnp.where(before, w, 0.0)
        block_total = jnp.broadcast_to(tail[:, 0:1] + first_keep, (2 * TQ_B, LANES))
        return w.astype(BF16), (block_total if carry is None else carry + block_total)

    def weighted_values(w, j):
        return _dot(w, v_ref[0, pl.ds(block_start(j), KB_B), :])

    def first_visits(tiles, tiles_with_previous):
        diag_block = {t: TILES_B * m + t for t in tiles}
        scored_diag = {t: scores(t, diag_block[t], True) for t in tiles}
        scored_prev = {t: scores(t, diag_block[t] - 1, False) for t in tiles_with_previous}
        w_diag, w_prev, carry = {}, {}, {}
        for t in tiles:
            w_diag[t], carry[t] = weights(scored_diag[t], None)
        for t in tiles_with_previous:
            w_prev[t], carry[t] = weights(scored_prev[t], carry[t])
        for t in tiles:
            pv = weighted_values(w_diag[t], diag_block[t])
            if t in tiles_with_previous:
                pv = pv + weighted_values(w_prev[t], diag_block[t] - 1)
            acc_ref[t] = pv
            carry_ref[t] = carry[t]
            top_ref[t] = jnp.max(carry[t])

    def first_visits_grouped(first_tile_has_previous):
        for t0 in range(0, TILES_B, GROUP_B):
            tiles = list(range(t0, min(t0 + GROUP_B, TILES_B)))
            first_visits(tiles, [t for t in tiles if t > 0 or first_tile_has_previous])

    @pl.when(m == 0)
    def _():
        first_visits_grouped(False)

    @pl.when(m > 0)
    def _():
        first_visits_grouped(True)

    def is_live(top):
        return (top >= CARRY_FLOOR_B).astype(jnp.int32)

    any_live = jnp.int32(0)
    for t in range(TILES_B):
        has_more_blocks = (TILES_B * m + t >= 2).astype(jnp.int32)
        any_live = jnp.maximum(any_live, has_more_blocks * is_live(top_ref[t]))

    @pl.when(any_live > 0)
    def _():
        for t in range(TILES_B):
            diag_block = TILES_B * m + t

            def more_keys(state, diag_block=diag_block):
                step, live = state
                return jnp.logical_and(step < diag_block, live > 0)

            def body(state, t=t, diag_block=diag_block):
                step, _ = state
                j = diag_block - 1 - step
                w, carry = weights(scores(t, j, False), carry_ref[t])
                acc_ref[t] += weighted_values(w, j)
                carry_ref[t] = carry
                return step + 1, is_live(jnp.max(carry))

            lax.while_loop(more_keys, body, (jnp.int32(1), is_live(top_ref[t])))

    for t in range(TILES_B):
        o_ref[0, t * TQ_B:(t + 1) * TQ_B, :] = jnp.where(
            low_lanes, acc_ref[t, 0:TQ_B, :], acc_ref[t, TQ_B:2 * TQ_B, :]).astype(BF16)


def _cumulative_matrix():
    j = np.arange(KB_B)[:, None]
    s = np.arange(KB_B)[None, :]
    return jnp.asarray((j > s).astype(np.float32), BF16)


def _mixer_b(qkvu3):
    b, s, _ = qkvu3.shape
    return pl.pallas_call(
        _mixer_b_kernel,
        grid=(b, N_PAIRS, s // (TILES_B * TQ_B)),
        in_specs=[
            pl.BlockSpec((1, TILES_B * TQ_B, LANES), lambda bi, p, i: (bi, i, 3 * N_PAIRS + p)),
            pl.BlockSpec((1, s, LANES), lambda bi, p, i: (bi, 0, 4 * N_PAIRS + p)),
            pl.BlockSpec((1, s, LANES), lambda bi, p, i: (bi, 0, 5 * N_PAIRS + p)),
            _resident((KB_B, KB_B)),
        ],
        out_specs=pl.BlockSpec((1, TILES_B * TQ_B, LANES), lambda bi, p, i: (bi, i, p)),
        out_shape=jax.ShapeDtypeStruct((b, s, WIDTH), BF16),
        scratch_shapes=[
            pltpu.VMEM((TILES_B, 2 * TQ_B, LANES), BF16),
            pltpu.VMEM((TILES_B, 2 * TQ_B, LANES), F32),
            pltpu.VMEM((TILES_B, 2 * TQ_B, LANES), F32),
            pltpu.SMEM((TILES_B,), F32),
        ],
        compiler_params=_compiler_params(("arbitrary", "arbitrary", "arbitrary")),
        name="mixer_b",
    )(qkvu3, qkvu3, qkvu3, _cumulative_matrix())


def _merge_kernel(tiles_per_seq, x_ref, oa_ref, ob_ref, u_ref, uh_ref, gate_ref,
                  wp_ref, ps_ref, wa_ref, wb_ref, wc_ref, wo_ref, o_ref):
    first = (pl.program_id(0) % tiles_per_seq) == 0
    u = u_ref[...].astype(F32)
    halo = jnp.where(first, 0.0, uh_ref[...].astype(F32))
    ext = jnp.concatenate([halo, u], axis=0)
    pos = lax.broadcasted_iota(jnp.int32, (TM_PROJ, 1), 0)

    pooled_groups = []
    for g, win in enumerate(POOL_WINDOWS):
        cols = slice(g * POOL_GROUP_DIM, (g + 1) * POOL_GROUP_DIM)
        a = ext[:, cols]
        step = 1
        while step < win:
            a = a + pltpu.roll(a, step, 0)
            step *= 2
        win_sum = a[HALO_POOL:]
        count = jnp.where(first, jnp.minimum(pos + 1, win), win).astype(F32)
        pooled = win_sum / count - u[:, cols]
        mixed = _dot(pooled.astype(BF16), wp_ref[g])
        pooled_groups.append(mixed * ps_ref[:, cols])
    o_c = jnp.concatenate(pooled_groups, axis=1).astype(BF16)

    merged = (gate_ref[:, 0:D_MODEL].astype(F32) * _dot(oa_ref[...], wa_ref[...])
              + gate_ref[:, D_MODEL:2 * D_MODEL].astype(F32) * _dot(ob_ref[...], wb_ref[...])
              + gate_ref[:, 2 * D_MODEL:3 * D_MODEL].astype(F32) * _dot(o_c, wc_ref[...]))
    o_ref[...] = x_ref[...] + _dot(merged.astype(BF16), wo_ref[...])


def _merge(x2, o_a, o_b, qkvu, gates, w_pool, pool_scale, w_a, w_b, w_c, w_out, seq, layer):
    t = x2.shape[0]
    u_block = QKVU_COLS // WIDTH - 1
    halo_blocks = TM_PROJ // HALO_POOL
    return pl.pallas_call(
        functools.partial(_merge_kernel, seq // TM_PROJ),
        grid=(t // TM_PROJ,),
        in_specs=[
            pl.BlockSpec((TM_PROJ, D_MODEL), lambda i: (i, 0)),
            pl.BlockSpec((TM_PROJ, WIDTH), lambda i: (i, 0)),
            pl.BlockSpec((TM_PROJ, WIDTH), lambda i: (i, 0)),
            pl.BlockSpec((TM_PROJ, WIDTH), lambda i: (i, u_block)),
            pl.BlockSpec((HALO_POOL, WIDTH), lambda i: (jnp.maximum(i * halo_blocks - 1, 0), u_block)),
            pl.BlockSpec((TM_PROJ, GATE_COLS), lambda i: (i, 0)),
            _layer_resident(layer, (len(POOL_WINDOWS), POOL_GROUP_DIM, POOL_GROUP_DIM)),
            _resident((1, WIDTH)),
            _layer_resident(layer, (WIDTH, D_MODEL)),
            _layer_resident(layer, (WIDTH, D_MODEL)),
            _layer_resident(layer, (WIDTH, D_MODEL)),
            _layer_resident(layer, (D_MODEL, D_MODEL)),
        ],
        out_specs=pl.BlockSpec((TM_PROJ, D_MODEL), lambda i: (i, 0)),
        out_shape=jax.ShapeDtypeStruct((t, D_MODEL), F32),
        compiler_params=_compiler_params(("arbitrary",)),
        name="merge",
    )(x2, o_a, o_b, qkvu, qkvu, gates, w_pool, pool_scale, w_a, w_b, w_c, w_out)


def _ffn_kernel(tiles_per_seq, x_ref, g_ref, wu_ref, cw_ref, cb_ref, wd_ref, o_ref,
                halo_ref, h_ref, acc_ref, act_ref):
    first = (pl.program_id(0) % tiles_per_seq) == 0
    groups = TM_FFN // SUBLANES
    x = jnp.swapaxes(x_ref[...].reshape(SUBLANES, groups, D_MODEL), 0, 1).reshape(TM_FFN, D_MODEL)
    h_ref[...] = _rms_norm_rows(x, g_ref[...]).astype(BF16)
    acc_ref[...] = x
    row = lax.broadcasted_iota(jnp.int32, (SUBLANES, 1), 0)

    def up(n0):
        return tuple(_dot(h_ref[...], wu_ref[:, c0:c0 + FF_CHUNK]) for c0 in (n0, D_FF + n0))

    def conv(u, c0):
        cols = slice(c0, c0 + FF_CHUNK)
        prev = jnp.where(first, 0.0, halo_ref[:, cols])
        halo_ref[:, cols] = u[TM_FFN - 2 * SUBLANES:]
        back2 = prev[SUBLANES - 1:SUBLANES]
        back1 = prev[2 * SUBLANES - 1:2 * SUBLANES]
        top1 = jnp.where(row == 0, back1, pltpu.roll(u[TM_FFN - SUBLANES:], 1, 0))
        top2 = jnp.where(row == 0, back2, pltpu.roll(u[TM_FFN - 2 * SUBLANES:TM_FFN - SUBLANES], 1, 0))
        u1 = jnp.concatenate([top1, u[:TM_FFN - SUBLANES]], axis=0)
        u2 = jnp.concatenate([top2, top1, u[:TM_FFN - 2 * SUBLANES]], axis=0)
        return (cb_ref[:, cols] + cw_ref[0:1, cols] * u2 + cw_ref[1:2, cols] * u1
                + cw_ref[2:3, cols] * u)

    chunks = list(range(0, D_FF, FF_CHUNK))
    u_next = up(chunks[0])
    for k, n0 in enumerate(chunks):
        u_gate, u_val = u_next
        if k + 1 < len(chunks):
            u_next = up(chunks[k + 1])
        act = (jax.nn.silu(conv(u_gate, n0)) * conv(u_val, D_FF + n0)).astype(BF16)
        act_ref[:, n0:n0 + FF_CHUNK] = act
    out = acc_ref[...] + _dot(act_ref[...], wd_ref[...])
    o_ref[...] = jnp.swapaxes(out.reshape(groups, SUBLANES, D_MODEL), 0, 1).reshape(TM_FFN, D_MODEL)


def _ffn(x2, gain, w_up, conv_w, conv_b, w_down, seq, layer):
    t = x2.shape[0]
    return pl.pallas_call(
        functools.partial(_ffn_kernel, seq // TM_FFN),
        grid=(t // TM_FFN,),
        in_specs=[
            pl.BlockSpec((TM_FFN, D_MODEL), lambda i: (i, 0)),
            _resident((1, D_MODEL)),
            _layer_resident(layer, (D_MODEL, 2 * D_FF)),
            _resident((3, 2 * D_FF)),
            _resident((1, 2 * D_FF)),
            _layer_resident(layer, (D_FF, D_MODEL)),
        ],
        out_specs=pl.BlockSpec((TM_FFN, D_MODEL), lambda i: (i, 0)),
        out_shape=jax.ShapeDtypeStruct((t, D_MODEL), F32),
        scratch_shapes=[pltpu.VMEM((2 * SUBLANES, 2 * D_FF), F32), pltpu.VMEM((TM_FFN, D_MODEL), BF16),
                        pltpu.VMEM((TM_FFN, D_MODEL), F32), pltpu.VMEM((TM_FFN, D_FF), BF16)],
        compiler_params=_compiler_params(("arbitrary",)),
        name="ffn",
    )(x2, gain, w_up, conv_w, conv_b, w_down)


def kernel(x, norm_mix, w_in, b_gate, q_norm_a, k_norm_a, rel_bias_a, w_pool, pool_scale,
           w_branch_a, w_branch_b, w_branch_c, w_out, norm_ffn, w_up, conv_w, conv_b, w_down):
    b, s, d = x.shape
    assert d == D_MODEL and s % TQ_A == 0 and s % TM_PROJ == 0 and s % (TILES_B * TQ_B) == 0
    depth = w_in.shape[0]
    w_in, w_pool, w_branch_a, w_branch_b, w_branch_c, w_out, w_up, w_down = (
        w.astype(BF16) for w in (w_in, w_pool, w_branch_a, w_branch_b, w_branch_c, w_out, w_up, w_down))
    bias_a = _rel_bias_units(rel_bias_a)
    x2 = x.reshape(b * s, d)
    for l in range(depth):
        gq = jnp.tile(q_norm_a[l], 2)[None, :]
        gk = jnp.tile(k_norm_a[l], 2)[None, :]
        qkvu, gates = _in_proj(x2, norm_mix[l][None, :], w_in, b_gate[l][None, :], gq, gk, l)
        qkvu3 = qkvu.reshape(b, s, QKVU_COLS)
        o_a = _mixer_a(qkvu3, bias_a, l)
        o_b = _mixer_b(qkvu3)
        x2 = _merge(x2, o_a.reshape(b * s, WIDTH), o_b.reshape(b * s, WIDTH), qkvu, gates,
                    w_pool, pool_scale[l][None, :], w_branch_a, w_branch_b, w_branch_c, w_out, s, l)
        x2 = _ffn(x2, norm_ffn[l][None, :], w_up, conv_w[l], conv_b[l][None, :], w_down, s, l)
    return x2.reshape(b, s, d)
```

```python
import functools
import math

import jax
import jax.numpy as jnp
import numpy as np
from jax import lax
from jax.experimental import pallas as pl
from jax.experimental.pallas import tpu as pltpu

D_MODEL = 1024
CHUNK = 64
N_LEFT_CHUNKS = 8
HEAD_DIM = 64
N_HEADS = 8
N_PAIRS = N_HEADS // 2
WIDTH = N_HEADS * HEAD_DIM
POOL_WINDOWS = (2, 4, 8, 16)
POOL_GROUP_DIM = 128
MAX_REL_DIST = 2 * CHUNK
D_FF = 2816
EPS = 1e-6
MASK_VALUE = -1e30

QKVU_COLS = 7 * WIDTH
GATE_COLS = 3 * D_MODEL

LANES = 128
SUBLANES = 8
VMEM_LIMIT_BYTES = 56 * 1024 * 1024

TM_PROJ = 1024
TN_PROJ = 512
TM_FFN = 1024
LOOKBACK_A = N_LEFT_CHUNKS * CHUNK
TQ_A = 2048
UNIT_A = 2 * CHUNK
BAND_A = (N_LEFT_CHUNKS + 2) * CHUNK
GROUP_A = 2
START_UNITS_A = LOOKBACK_A // UNIT_A
TQ_B = 256
TILES_B = 8
GROUP_B = 8
KB_B = 256
FF_CHUNK = 256
HALO_POOL = 16

LOG2E = 1.4426950408889634
CARRY_FLOOR_B = -152.0

BF16 = jnp.bfloat16
F32 = jnp.float32


def _dot(a, b):
    return jnp.dot(a, b, preferred_element_type=F32)


def _dot_nt(a, b):
    return lax.dot_general(a, b, (((1,), (1,)), ((), ())), preferred_element_type=F32)


def _rms_norm_rows(x, gain):
    ms = jnp.mean(x * x, axis=-1, keepdims=True)
    return x * lax.rsqrt(ms + EPS) * gain


def _compiler_params(semantics):
    return pltpu.CompilerParams(dimension_semantics=semantics,
                                vmem_limit_bytes=VMEM_LIMIT_BYTES)


def _resident(shape):
    zeros = (0,) * len(shape)
    return pl.BlockSpec(shape, lambda *_: zeros, pipeline_mode=pl.Buffered(1))


def _layer_resident(layer, shape):
    index = (layer,) + (0,) * len(shape)
    return pl.BlockSpec((None,) + tuple(shape), lambda *_: index, pipeline_mode=pl.Buffered(1))


def _head_pair_norm(x, gain, low_lanes):
    sq = x * x
    s_low = jnp.sum(jnp.where(low_lanes, sq, 0.0), axis=-1, keepdims=True)
    s_high = jnp.sum(jnp.where(low_lanes, 0.0, sq), axis=-1, keepdims=True)
    ms = jnp.where(low_lanes, s_low, s_high) * (1.0 / HEAD_DIM)
    return x * lax.rsqrt(ms + EPS) * gain


def _low_lanes():
    return lax.broadcasted_iota(jnp.int32, (1, LANES), 1) < HEAD_DIM


def _in_proj_kernel(x_ref, g_ref, w_ref, b_ref, gq_ref, gk_ref, qkvu_ref, gate_ref):
    h = _rms_norm_rows(x_ref[...], g_ref[...]).astype(BF16)
    low_lanes = _low_lanes()
    score_scale = 1.0 / math.sqrt(HEAD_DIM)
    head_norm = {0: gq_ref, WIDTH: gk_ref}
    scaled = {0: score_scale, 3 * WIDTH: score_scale}
    for n0 in range(0, QKVU_COLS, TN_PROJ):
        y = _dot(h, w_ref[:, n0:n0 + TN_PROJ])
        if n0 in head_norm:
            y = jnp.concatenate(
                [_head_pair_norm(y[:, c:c + LANES], head_norm[n0][...], low_lanes)
                 for c in range(0, TN_PROJ, LANES)], axis=1)
        if n0 in scaled:
            y = y * scaled[n0]
        qkvu_ref[:, n0:n0 + TN_PROJ] = y.astype(BF16)
    for n0 in range(0, GATE_COLS, TN_PROJ):
        z = _dot(h, w_ref[:, QKVU_COLS + n0:QKVU_COLS + n0 + TN_PROJ]) + b_ref[:, n0:n0 + TN_PROJ]
        gate_ref[:, n0:n0 + TN_PROJ] = (0.5 * jnp.tanh(0.5 * z) + 0.5).astype(BF16)


def _in_proj(x2, gain, w_in, b_gate, gq, gk, layer):
    t = x2.shape[0]
    return pl.pallas_call(
        _in_proj_kernel,
        grid=(t // TM_PROJ,),
        in_specs=[
            pl.BlockSpec((TM_PROJ, D_MODEL), lambda i: (i, 0)),
            _resident((1, D_MODEL)),
            _layer_resident(layer, (D_MODEL, QKVU_COLS + GATE_COLS)),
            _resident((1, GATE_COLS)),
            _resident((1, LANES)),
            _resident((1, LANES)),
        ],
        out_specs=[
            pl.BlockSpec((TM_PROJ, QKVU_COLS), lambda i: (i, 0)),
            pl.BlockSpec((TM_PROJ, GATE_COLS), lambda i: (i, 0)),
        ],
        out_shape=[
            jax.ShapeDtypeStruct((t, QKVU_COLS), BF16),
            jax.ShapeDtypeStruct((t, GATE_COLS), BF16),
        ],
        compiler_params=_compiler_params(("arbitrary",)),
        name="in_proj",
    )(x2, gain, w_in, b_gate, gq, gk)


def _mixer_a_kernel(q_ref, k_ref, v_ref, bias_ref, o_ref, q_scr, kt_scr, v_scr):
    t = pl.program_id(2)
    low_lanes = _low_lanes()
    n_units = TQ_A // UNIT_A

    zero = jnp.zeros((), BF16)
    for u in range(n_units):
        qu = q_ref[0, u * UNIT_A:(u + 1) * UNIT_A, :]
        q_scr[u, 0:UNIT_A, :] = jnp.where(low_lanes, qu, zero)
        q_scr[u, UNIT_A:2 * UNIT_A, :] = jnp.where(low_lanes, zero, qu)

    @pl.when(t == 0)
    def _():
        kt_scr[:, 0:LOOKBACK_A] = jnp.zeros((LANES, LOOKBACK_A), BF16)
        v_scr[0:LOOKBACK_A, :] = jnp.zeros((LOOKBACK_A, LANES), BF16)

    @pl.when(t > 0)
    def _():
        kt_scr[:, 0:LOOKBACK_A] = kt_scr[:, TQ_A:TQ_A + LOOKBACK_A]
        v_scr[0:LOOKBACK_A, :] = v_scr[TQ_A:TQ_A + LOOKBACK_A, :]

    kt_scr[:, LOOKBACK_A:LOOKBACK_A + TQ_A] = k_ref[0].astype(F32).T.astype(BF16)
    v_scr[LOOKBACK_A:LOOKBACK_A + TQ_A, :] = v_ref[0]

    for u0 in range(0, n_units, GROUP_A):
        group = range(u0, min(u0 + GROUP_A, n_units))
        probs, denoms = {}, {}
        for u in group:
            k0 = u * UNIT_A
            variant = jnp.where(t == 0, u + 1, 0) if u < START_UNITS_A else 0
            bias = jnp.concatenate([bias_ref[0, variant], bias_ref[1, variant]], axis=0)
            logits = _dot(q_scr[u], kt_scr[:, k0:k0 + BAND_A]) + bias
            m = jnp.max(logits, axis=-1, keepdims=True)
            p = jnp.exp(logits - m)
            denoms[u] = jnp.sum(p, axis=-1, keepdims=True)
            probs[u] = p.astype(BF16)
        for u in group:
            k0 = u * UNIT_A
            pv = _dot(probs[u], v_scr[k0:k0 + BAND_A, :]) / denoms[u]
            o_ref[0, k0:k0 + UNIT_A, :] = jnp.where(low_lanes, pv[0:UNIT_A], pv[UNIT_A:]).astype(BF16)


def _mixer_a(qkvu3, bias, layer):
    b, s, _ = qkvu3.shape
    blk = (1, TQ_A, LANES)
    return pl.pallas_call(
        _mixer_a_kernel,
        grid=(b, N_PAIRS, s // TQ_A),
        in_specs=[
            pl.BlockSpec(blk, lambda bi, p, t: (bi, t, p)),
            pl.BlockSpec(blk, lambda bi, p, t: (bi, t, N_PAIRS + p)),
            pl.BlockSpec(blk, lambda bi, p, t: (bi, t, 2 * N_PAIRS + p)),
            pl.BlockSpec((None, 2, 1 + START_UNITS_A, UNIT_A, BAND_A), lambda bi, p, t: (layer, p, 0, 0, 0)),
        ],
        out_specs=pl.BlockSpec(blk, lambda bi, p, t: (bi, t, p)),
        out_shape=jax.ShapeDtypeStruct((b, s, WIDTH), BF16),
        scratch_shapes=[
            pltpu.VMEM((TQ_A // UNIT_A, 2 * UNIT_A, LANES), BF16),
            pltpu.VMEM((LANES, LOOKBACK_A + TQ_A), BF16),
            pltpu.VMEM((LOOKBACK_A + TQ_A, LANES), BF16),
        ],
        compiler_params=_compiler_params(("arbitrary", "arbitrary", "arbitrary")),
        name="mixer_a",
    )(qkvu3, qkvu3, qkvu3, bias)


def _rel_bias_units(rel_bias):
    period = UNIT_A + BAND_A
    d = np.arange(period)
    d = np.where(d < BAND_A, d, d - period)
    idx = np.clip(N_LEFT_CHUNKS * CHUNK - d, -(CHUNK - 1), MAX_REL_DIST) + (CHUNK - 1)
    vec = rel_bias.astype(F32)[..., idx]
    lead = rel_bias.shape[:-1]
    rows = jnp.tile(vec, (1, 1, UNIT_A))[..., :UNIT_A * (period - 1)]
    toeplitz = rows.reshape(*lead, UNIT_A, period - 1)[..., :BAND_A]
    q_chunk = np.arange(UNIT_A)[:, None] // CHUNK
    k_chunk = np.arange(BAND_A)[None, :] // CHUNK
    in_band = (k_chunk >= q_chunk) & (k_chunk <= q_chunk + N_LEFT_CHUNKS)
    key = np.arange(BAND_A)[None, :]
    keep = np.stack([in_band] + [in_band & (key + u * UNIT_A >= LOOKBACK_A) for u in range(START_UNITS_A)])
    return jnp.where(keep, toeplitz[..., None, :, :], MASK_VALUE)


def _mixer_b_kernel(q_ref, k_ref, v_ref, cum_ref, o_ref, q_scr, acc_ref, carry_ref, top_ref):
    m = pl.program_id(2)
    low_lanes = _low_lanes()
    zero = jnp.zeros((), BF16)
    for tile in range(TILES_B):
        q = q_ref[0, tile * TQ_B:(tile + 1) * TQ_B, :]
        q_scr[tile, 0:TQ_B, :] = jnp.where(low_lanes, q, zero)
        q_scr[tile, TQ_B:2 * TQ_B, :] = jnp.where(low_lanes, zero, q)

    def block_start(j):
        return pl.multiple_of(j * KB_B, KB_B)

    def scores(tile, j, diagonal):
        neg_z = _dot_nt(q_scr[tile], k_ref[0, pl.ds(block_start(j), KB_B), :]) * (-LOG2E)
        soft = jnp.log2(1.0 + jnp.exp2(-jnp.abs(neg_z)))
        log_keep = jnp.minimum(neg_z, 0.0) - soft
        log_beta = log_keep - neg_z
        before = None
        if diagonal:
            row = lax.broadcasted_iota(jnp.int32, (2 * TQ_B, KB_B), 0)
            col = lax.broadcasted_iota(jnp.int32, (2 * TQ_B, KB_B), 1)
            before = col < jnp.where(row >= TQ_B, row - TQ_B, row)
            log_keep = jnp.where(before, log_keep, 0.0)
        return log_keep.astype(BF16), log_beta, log_keep[:, 0:1], before

    def weights(scored, carry):
        keep, log_beta, first_keep, before = scored
        tail = _dot(keep, cum_ref[...])
        exponent = log_beta + tail
        if carry is not None:
            exponent = exponent + jnp.concatenate([carry] * (KB_B // LANES), axis=1)
        w = jnp.exp2(exponent)
        if before is not None:
            w = jnp.where(before, w, 0.0)
        block_total = jnp.broadcast_to(tail[:, 0:1] + first_keep, (2 * TQ_B, LANES))
        return w.astype(BF16), (block_total if carry is None else carry + block_total)

    def weighted_values(w, j):
        return _dot(w, v_ref[0, pl.ds(block_start(j), KB_B), :])

    def first_visits(tiles, tiles_with_previous):
        diag_block = {t: TILES_B * m + t for t in tiles}
        scored_diag = {t: scores(t, diag_block[t], True) for t in tiles}
        scored_prev = {t: scores(t, diag_block[t] - 1, False) for t in tiles_with_previous}
        w_diag, w_prev, carry = {}, {}, {}
        for t in tiles:
            w_diag[t], carry[t] = weights(scored_diag[t], None)
        for t in tiles_with_previous:
            w_prev[t], carry[t] = weights(scored_prev[t], carry[t])
        for t in tiles:
            pv = weighted_values(w_diag[t], diag_block[t])
            if t in tiles_with_previous:
                pv = pv + weighted_values(w_prev[t], diag_block[t] - 1)
            acc_ref[t] = pv
            carry_ref[t] = carry[t]
            top_ref[t] = jnp.max(carry[t])

    def first_visits_grouped(first_tile_has_previous):
        for t0 in range(0, TILES_B, GROUP_B):
            tiles = list(range(t0, min(t0 + GROUP_B, TILES_B)))
            first_visits(tiles, [t for t in tiles if t > 0 or first_tile_has_previous])

    @pl.when(m == 0)
    def _():
        first_visits_grouped(False)

    @pl.when(m > 0)
    def _():
        first_visits_grouped(True)

    def is_live(top):
        return (top >= CARRY_FLOOR_B).astype(jnp.int32)

    any_live = jnp.int32(0)
    for t in range(TILES_B):
        has_more_blocks = (TILES_B * m + t >= 2).astype(jnp.int32)
        any_live = jnp.maximum(any_live, has_more_blocks * is_live(top_ref[t]))

    @pl.when(any_live > 0)
    def _():
        for t in range(TILES_B):
            diag_block = TILES_B * m + t

            def more_keys(state, diag_block=diag_block):
                step, live = state
                return jnp.logical_and(step < diag_block, live > 0)

            def body(state, t=t, diag_block=diag_block):
                step, _ = state
                j = diag_block - 1 - step
                w, carry = weights(scores(t, j, False), carry_ref[t])
                acc_ref[t] += weighted_values(w, j)
                carry_ref[t] = carry
                return step + 1, is_live(jnp.max(carry))

            lax.while_loop(more_keys, body, (jnp.int32(1), is_live(top_ref[t])))

    for t in range(TILES_B):
        o_ref[0, t * TQ_B:(t + 1) * TQ_B, :] = jnp.where(
            low_lanes, acc_ref[t, 0:TQ_B, :], acc_ref[t, TQ_B:2 * TQ_B, :]).astype(BF16)


def _cumulative_matrix():
    j = np.arange(KB_B)[:, None]
    s = np.arange(KB_B)[None, :]
    return jnp.asarray((j > s).astype(np.float32), BF16)


def _mixer_b(qkvu3):
    b, s, _ = qkvu3.shape
    return pl.pallas_call(
        _mixer_b_kernel,
        grid=(b, N_PAIRS, s // (TILES_B * TQ_B)),
        in_specs=[
            pl.BlockSpec((1, TILES_B * TQ_B, LANES), lambda bi, p, i: (bi, i, 3 * N_PAIRS + p)),
            pl.BlockSpec((1, s, LANES), lambda bi, p, i: (bi, 0, 4 * N_PAIRS + p)),
            pl.BlockSpec((1, s, LANES), lambda bi, p, i: (bi, 0, 5 * N_PAIRS + p)),
            _resident((KB_B, KB_B)),
        ],
        out_specs=pl.BlockSpec((1, TILES_B * TQ_B, LANES), lambda bi, p, i: (bi, i, p)),
        out_shape=jax.ShapeDtypeStruct((b, s, WIDTH), BF16),
        scratch_shapes=[
            pltpu.VMEM((TILES_B, 2 * TQ_B, LANES), BF16),
            pltpu.VMEM((TILES_B, 2 * TQ_B, LANES), F32),
            pltpu.VMEM((TILES_B, 2 * TQ_B, LANES), F32),
            pltpu.SMEM((TILES_B,), F32),
        ],
        compiler_params=_compiler_params(("arbitrary", "arbitrary", "arbitrary")),
        name="mixer_b",
    )(qkvu3, qkvu3, qkvu3, _cumulative_matrix())


def _merge_kernel(tiles_per_seq, x_ref, oa_ref, ob_ref, u_ref, uh_ref, gate_ref,
                  wp_ref, ps_ref, wa_ref, wb_ref, wc_ref, wo_ref, o_ref):
    first = (pl.program_id(0) % tiles_per_seq) == 0
    u = u_ref[...].astype(F32)
    halo = jnp.where(first, 0.0, uh_ref[...].astype(F32))
    ext = jnp.concatenate([halo, u], axis=0)
    pos = lax.broadcasted_iota(jnp.int32, (TM_PROJ, 1), 0)

    pooled_groups = []
    for g, win in enumerate(POOL_WINDOWS):
        cols = slice(g * POOL_GROUP_DIM, (g + 1) * POOL_GROUP_DIM)
        a = ext[:, cols]
        step = 1
        while step < win:
            a = a + pltpu.roll(a, step, 0)
            step *= 2
        win_sum = a[HALO_POOL:]
        count = jnp.where(first, jnp.minimum(pos + 1, win), win).astype(F32)
        pooled = win_sum / count - u[:, cols]
        mixed = _dot(pooled.astype(BF16), wp_ref[g])
        pooled_groups.append(mixed * ps_ref[:, cols])
    o_c = jnp.concatenate(pooled_groups, axis=1).astype(BF16)

    merged = (gate_ref[:, 0:D_MODEL].astype(F32) * _dot(oa_ref[...], wa_ref[...])
              + gate_ref[:, D_MODEL:2 * D_MODEL].astype(F32) * _dot(ob_ref[...], wb_ref[...])
              + gate_ref[:, 2 * D_MODEL:3 * D_MODEL].astype(F32) * _dot(o_c, wc_ref[...]))
    o_ref[...] = x_ref[...] + _dot(merged.astype(BF16), wo_ref[...])


def _merge(x2, o_a, o_b, qkvu, gates, w_pool, pool_scale, w_a, w_b, w_c, w_out, seq, layer):
    t = x2.shape[0]
    u_block = QKVU_COLS // WIDTH - 1
    halo_blocks = TM_PROJ // HALO_POOL
    return pl.pallas_call(
        functools.partial(_merge_kernel, seq // TM_PROJ),
        grid=(t // TM_PROJ,),
        in_specs=[
            pl.BlockSpec((TM_PROJ, D_MODEL), lambda i: (i, 0)),
            pl.BlockSpec((TM_PROJ, WIDTH), lambda i: (i, 0)),
            pl.BlockSpec((TM_PROJ, WIDTH), lambda i: (i, 0)),
            pl.BlockSpec((TM_PROJ, WIDTH), lambda i: (i, u_block)),
            pl.BlockSpec((HALO_POOL, WIDTH), lambda i: (jnp.maximum(i * halo_blocks - 1, 0), u_block)),
            pl.BlockSpec((TM_PROJ, GATE_COLS), lambda i: (i, 0)),
            _layer_resident(layer, (len(POOL_WINDOWS), POOL_GROUP_DIM, POOL_GROUP_DIM)),
            _resident((1, WIDTH)),
            _layer_resident(layer, (WIDTH, D_MODEL)),
            _layer_resident(layer, (WIDTH, D_MODEL)),
            _layer_resident(layer, (WIDTH, D_MODEL)),
            _layer_resident(layer, (D_MODEL, D_MODEL)),
        ],
        out_specs=pl.BlockSpec((TM_PROJ, D_MODEL), lambda i: (i, 0)),
        out_shape=jax.ShapeDtypeStruct((t, D_MODEL), F32),
        compiler_params=_compiler_params(("arbitrary",)),
        name="merge",
    )(x2, o_a, o_b, qkvu, qkvu, gates, w_pool, pool_scale, w_a, w_b, w_c, w_out)


def _ffn_kernel(tiles_per_seq, x_ref, g_ref, wu_ref, cw_ref, cb_ref, wd_ref, o_ref,
                halo_ref, h_ref, acc_ref, act_ref):
    first = (pl.program_id(0) % tiles_per_seq) == 0
    groups = TM_FFN // SUBLANES
    x = jnp.swapaxes(x_ref[...].reshape(SUBLANES, groups, D_MODEL), 0, 1).reshape(TM_FFN, D_MODEL)
    h_ref[...] = _rms_norm_rows(x, g_ref[...]).astype(BF16)
    acc_ref[...] = x
    row = lax.broadcasted_iota(jnp.int32, (SUBLANES, 1), 0)

    def up(n0):
        return tuple(_dot(h_ref[...], wu_ref[:, c0:c0 + FF_CHUNK]) for c0 in (n0, D_FF + n0))

    def conv(u, c0):
        cols = slice(c0, c0 + FF_CHUNK)
        prev = jnp.where(first, 0.0, halo_ref[:, cols])
        halo_ref[:, cols] = u[TM_FFN - 2 * SUBLANES:]
        back2 = prev[SUBLANES - 1:SUBLANES]
        back1 = prev[2 * SUBLANES - 1:2 * SUBLANES]
        top1 = jnp.where(row == 0, back1, pltpu.roll(u[TM_FFN - SUBLANES:], 1, 0))
        top2 = jnp.where(row == 0, back2, pltpu.roll(u[TM_FFN - 2 * SUBLANES:TM_FFN - SUBLANES], 1, 0))
        u1 = jnp.concatenate([top1, u[:TM_FFN - SUBLANES]], axis=0)
        u2 = jnp.concatenate([top2, top1, u[:TM_FFN - 2 * SUBLANES]], axis=0)
        return (cb_ref[:, cols] + cw_ref[0:1, cols] * u2 + cw_ref[1:2, cols] * u1
                + cw_ref[2:3, cols] * u)

    chunks = list(range(0, D_FF, FF_CHUNK))
    u_next = up(chunks[0])
    for k, n0 in enumerate(chunks):
        u_gate, u_val = u_next
        if k + 1 < len(chunks):
            u_next = up(chunks[k + 1])
        act = (jax.nn.silu(conv(u_gate, n0)) * conv(u_val, D_FF + n0)).astype(BF16)
        act_ref[:, n0:n0 + FF_CHUNK] = act
    out = acc_ref[...] + _dot(act_ref[...], wd_ref[...])
    o_ref[...] = jnp.swapaxes(out.reshape(groups, SUBLANES, D_MODEL), 0, 1).reshape(TM_FFN, D_MODEL)


def _ffn(x2, gain, w_up, conv_w, conv_b, w_down, seq, layer):
    t = x2.shape[0]
    return pl.pallas_call(
        functools.partial(_ffn_kernel, seq // TM_FFN),
        grid=(t // TM_FFN,),
        in_specs=[
            pl.BlockSpec((TM_FFN, D_MODEL), lambda i: (i, 0)),
            _resident((1, D_MODEL)),
            _layer_resident(layer, (D_MODEL, 2 * D_FF)),
            _resident((3, 2 * D_FF)),
            _resident((1, 2 * D_FF)),
            _layer_resident(layer, (D_FF, D_MODEL)),
        ],
        out_specs=pl.BlockSpec((TM_FFN, D_MODEL), lambda i: (i, 0)),
        out_shape=jax.ShapeDtypeStruct((t, D_MODEL), F32),
        scratch_shapes=[pltpu.VMEM((2 * SUBLANES, 2 * D_FF), F32), pltpu.VMEM((TM_FFN, D_MODEL), BF16),
                        pltpu.VMEM((TM_FFN, D_MODEL), F32), pltpu.VMEM((TM_FFN, D_FF), BF16)],
        compiler_params=_compiler_params(("arbitrary",)),
        name="ffn",
    )(x2, gain, w_up, conv_w, conv_b, w_down)


def kernel(x, norm_mix, w_in, b_gate, q_norm_a, k_norm_a, rel_bias_a, w_pool, pool_scale,
           w_branch_a, w_branch_b, w_branch_c, w_out, norm_ffn, w_up, conv_w, conv_b, w_down):
    b, s, d = x.shape
    assert d == D_MODEL and s % TQ_A == 0 and s % TM_PROJ == 0 and s % (TILES_B * TQ_B) == 0
    depth = w_in.shape[0]
    w_in, w_pool, w_branch_a, w_branch_b, w_branch_c, w_out, w_up, w_down = (
        w.astype(BF16) for w in (w_in, w_pool, w_branch_a, w_branch_b, w_branch_c, w_out, w_up, w_down))
    bias_a = _rel_bias_units(rel_bias_a)
    x2 = x.reshape(b * s, d)
    for l in range(depth):
        gq = jnp.tile(q_norm_a[l], 2)[None, :]
        gk = jnp.tile(k_norm_a[l], 2)[None, :]
        qkvu, gates = _in_proj(x2, norm_mix[l][None, :], w_in, b_gate[l][None, :], gq, gk, l)
        qkvu3 = qkvu.reshape(b, s, QKVU_COLS)
        o_a = _mixer_a(qkvu3, bias_a, l)
        o_b = _mixer_b(qkvu3)
        x2 = _merge(x2, o_a.reshape(b * s, WIDTH), o_b.reshape(b * s, WIDTH), qkvu, gates,
                    w_pool, pool_scale[l][None, :], w_branch_a, w_branch_b, w_branch_c, w_out, s, l)
        x2 = _ffn(x2, norm_ffn[l][None, :], w_up, conv_w[l], conv_b[l][None, :], w_down, s, l)
    return x2.reshape(b, s, d)
```
